```python
import jax, jax.numpy as jnp
from jax import lax
import numpy as np

D_MODEL = 1024
BATCH = 2
SEQ = 8192
DEPTH = 4

GRID_W = 64
BLOCK = 128
EPS = 1e-6
ROPE_THETA = 10000.0

A_GROUPS = 8
A_DIM = 64
A_WIDTH = A_GROUPS * A_DIM
B_HEADS = 8
B_KV_HEADS = 2
B_HEAD_DIM = 64
B_WIDTH = B_HEADS * B_HEAD_DIM
C_HEADS = 8
C_NOPE = 64
C_ROPE = 32
C_V = 64
C_Q_RANK = 256
C_KV_RANK = 128
C_WIDTH = C_HEADS * C_V
COLS_A = 2 * A_WIDTH
COLS_B = (B_HEADS + 2 * B_KV_HEADS) * B_HEAD_DIM
COLS_C = C_Q_RANK + C_KV_RANK + C_ROPE
N_BRANCH = 3
COLS_GATE = N_BRANCH * D_MODEL
D_IN = COLS_A + COLS_B + COLS_C + COLS_GATE
PEER_HEADS = 8
PEER_N_KEYS = 128
PEER_N_EXPERTS = PEER_N_KEYS * PEER_N_KEYS
PEER_D_KEY = 256
PEER_TOPK = 16
PEER_CHUNK = 128

kernel_name = 'hybrid_gated_mixers_peer_encoder'


def rms_norm(x, g):
    xf = x.astype(jnp.float32)
    y = xf * lax.rsqrt(jnp.mean(xf * xf, axis=-1, keepdims=True) + EPS)
    return (y * g.astype(jnp.float32)).astype(x.dtype)


def modulate(x, shift, scale):
    return x * (1 + scale[:, None, :]) + shift[:, None, :]


def axial_rope_angles(seq, d_rot):
    rows = seq // GRID_W
    row = jnp.repeat(jnp.arange(rows, dtype=jnp.float32), GRID_W)
    col = jnp.tile(jnp.arange(GRID_W, dtype=jnp.float32), rows)
    half = d_rot // 2
    freq = ROPE_THETA ** (-jnp.arange(0, half, 2, dtype=jnp.float32) / half)
    ang = jnp.concatenate([row[:, None] * freq, col[:, None] * freq], axis=-1)
    return jnp.cos(ang), jnp.sin(ang)


def apply_rope(x, cos, sin):
    xf = x.astype(jnp.float32).reshape(*x.shape[:-1], -1, 2)
    x1, x2 = xf[..., 0], xf[..., 1]
    c = cos[None, :, None, :]
    s = sin[None, :, None, :]
    out = jnp.stack([x1 * c - x2 * s, x1 * s + x2 * c], axis=-1).reshape(x.shape)
    return out.astype(x.dtype)


def block_attention(q, k, v):
    b, s, hq, dk = q.shape
    hkv = k.shape[2]
    grp = hq // hkv
    scale = dk ** -0.5
    qb = q.reshape(b, s // BLOCK, BLOCK, hkv, grp, dk).transpose(1, 0, 2, 3, 4, 5)

    def one_block(qblk):
        sc = jnp.einsum('bqkgd,bskd->bkgqs', qblk, k, preferred_element_type=jnp.float32) * scale
        p = jax.nn.softmax(sc, axis=-1)
        return jnp.einsum('bkgqs,bskd->bqkgd', p.astype(v.dtype), v)

    out = lax.map(one_block, qb)
    return out.transpose(1, 0, 2, 3, 4, 5).reshape(b, s, hq * v.shape[-1])


def spatial_gating_branch(za, v_gain, w_s, b_s):
    b, s, _ = za.shape
    z = jax.nn.gelu(za)
    u, v = jnp.split(z, 2, axis=-1)
    v = rms_norm(v, v_gain).reshape(b, s // BLOCK, BLOCK, A_GROUPS, A_DIM)
    mixed = jnp.einsum('gpq,bcqgd->bcpgd', w_s, v) + b_s.T[None, None, :, :, None]
    return u * mixed.reshape(b, s, A_WIDTH)


def gqa_branch(zb, q_gain, k_gain, cos, sin):
    b, s, _ = zb.shape
    q, k, v = jnp.split(zb, [B_HEADS * B_HEAD_DIM, (B_HEADS + B_KV_HEADS) * B_HEAD_DIM], axis=-1)
    q = apply_rope(rms_norm(q.reshape(b, s, B_HEADS, B_HEAD_DIM), q_gain), cos, sin)
    k = apply_rope(rms_norm(k.reshape(b, s, B_KV_HEADS, B_HEAD_DIM), k_gain), cos, sin)
    v = v.reshape(b, s, B_KV_HEADS, B_HEAD_DIM)
    return block_attention(q, k, v)


def mla_branch(zc, cq_gain, w_uq, ckv_gain, w_ukv, cos, sin):
    b, s, _ = zc.shape
    c_q, c_kv, k_rope = jnp.split(zc, [C_Q_RANK, C_Q_RANK + C_KV_RANK], axis=-1)
    q = (rms_norm(c_q, cq_gain) @ w_uq).reshape(b, s, C_HEADS, C_NOPE + C_ROPE)
    q_nope, q_rope = jnp.split(q, [C_NOPE], axis=-1)
    kv = (rms_norm(c_kv, ckv_gain) @ w_ukv).reshape(b, s, C_HEADS, C_NOPE + C_V)
    k_nope, v = jnp.split(kv, [C_NOPE], axis=-1)
    k_rope = apply_rope(k_rope[:, :, None, :], cos, sin)
    q = jnp.concatenate([q_nope, apply_rope(q_rope, cos, sin)], axis=-1)
    k = jnp.concatenate([k_nope, jnp.broadcast_to(k_rope, (b, s, C_HEADS, C_ROPE))], axis=-1)
    return block_attention(q, k, v)


def peer_ffn(h, w_query, sub_keys, u_table, v_table):
    b, s, d = h.shape
    t = h.reshape(-1, PEER_CHUNK, d)

    def one_chunk(xc):
        q = (xc @ w_query).reshape(PEER_CHUNK, PEER_HEADS, 2, PEER_D_KEY // 2)
        sc = jnp.einsum('thpd,hpkd->thpk', q, sub_keys, preferred_element_type=jnp.float32)
        top_v, top_i = lax.top_k(sc, PEER_TOPK)
        cand_v = (top_v[:, :, 0, :, None] + top_v[:, :, 1, None, :]).reshape(PEER_CHUNK, PEER_HEADS, -1)
        cand_i = (top_i[:, :, 0, :, None] * PEER_N_KEYS + top_i[:, :, 1, None, :]).reshape(PEER_CHUNK, PEER_HEADS, -1)
        best_v, best_pos = lax.top_k(cand_v, PEER_TOPK)
        ids = jnp.take_along_axis(cand_i, best_pos, axis=-1)
        g = jax.nn.softmax(best_v, axis=-1)
        act = jax.nn.gelu(jnp.einsum('thkd,td->thk', u_table[ids], xc))
        w = (g * act.astype(jnp.float32)).astype(xc.dtype)
        return jnp.einsum('thk,thkd->td', w, v_table[ids])

    return lax.map(one_chunk, t).reshape(b, s, d)


def setup_inputs(seed: int = 0) -> dict:
    key = jax.random.key(seed)
    ks = jax.random.split(key, 25)
    L, D = DEPTH, D_MODEL

    def nrm(k, shape, scale):
        return jax.random.normal(k, shape, jnp.float32) * scale

    def gain(k, shape):
        return 1.0 + 0.02 * jax.random.normal(k, shape, jnp.float32)

    return {
        'x': nrm(ks[0], (BATCH, SEQ, D), 1.0),
        'c': nrm(ks[1], (BATCH, D), 1.0),
        'w_ada': nrm(ks[2], (L, D, 6 * D), 0.5 * D ** -0.5),
        'b_ada': nrm(ks[3], (L, 6 * D), 0.02),
        'norm1_gain': gain(ks[4], (L, D)),
        'w_in': nrm(ks[5], (L, D, D_IN), D ** -0.5),
        'a_v_gain': gain(ks[6], (L, A_WIDTH)),
        'a_w_s': nrm(ks[7], (L, A_GROUPS, BLOCK, BLOCK), 0.5 * BLOCK ** -0.5),
        'a_b_s': gain(ks[8], (L, A_GROUPS, BLOCK)),
        'b_q_gain': gain(ks[9], (L, B_HEAD_DIM)),
        'b_k_gain': gain(ks[10], (L, B_HEAD_DIM)),
        'c_q_gain': gain(ks[11], (L, C_Q_RANK)),
        'c_w_uq': nrm(ks[12], (L, C_Q_RANK, C_HEADS * (C_NOPE + C_ROPE)), C_Q_RANK ** -0.5),
        'c_kv_gain': gain(ks[13], (L, C_KV_RANK)),
        'c_w_ukv': nrm(ks[14], (L, C_KV_RANK, C_HEADS * (C_NOPE + C_V)), C_KV_RANK ** -0.5),
        'w_pa': nrm(ks[15], (L, A_WIDTH, D), A_WIDTH ** -0.5),
        'w_pb': nrm(ks[16], (L, B_WIDTH, D), B_WIDTH ** -0.5),
        'w_pc': nrm(ks[17], (L, C_WIDTH, D), C_WIDTH ** -0.5),
        'w_out': nrm(ks[18], (L, D, D), D ** -0.5),
        'norm2_gain': gain(ks[19], (L, D)),
        'peer_w_query': nrm(ks[20], (L, D, PEER_HEADS * PEER_D_KEY), D ** -0.5),
        'peer_sub_keys': nrm(ks[21], (L, PEER_HEADS, 2, PEER_N_KEYS, PEER_D_KEY // 2), (PEER_D_KEY // 2) ** -0.5),
        'peer_u': nrm(ks[22], (L, PEER_N_EXPERTS, D), D ** -0.5),
        'peer_v': nrm(ks[23], (L, PEER_N_EXPERTS, D), 0.5),
        'final_gain': gain(ks[24], (D,)),
    }


def reference(x, c, w_ada, b_ada, norm1_gain, w_in, a_v_gain, a_w_s, a_b_s, b_q_gain, b_k_gain,
              c_q_gain, c_w_uq, c_kv_gain, c_w_ukv, w_pa, w_pb, w_pc, w_out, norm2_gain,
              peer_w_query, peer_sub_keys, peer_u, peer_v, final_gain):
    b, seq, d = x.shape
    cos_b, sin_b = axial_rope_angles(seq, B_HEAD_DIM)
    cos_c, sin_c = axial_rope_angles(seq, C_ROPE)
    cos_b, sin_b, cos_c, sin_c = [t.astype(x.dtype) for t in (cos_b, sin_b, cos_c, sin_c)]
    c_act = jax.nn.silu(c)
    for l in range(DEPTH):
        mod = c_act @ w_ada[l] + b_ada[l]
        sh1, sc1, g1, sh2, sc2, g2 = jnp.split(mod, 6, axis=-1)
        h = modulate(rms_norm(x, norm1_gain[l]), sh1, sc1)
        z = h @ w_in[l]
        za, zb, zc, zg = jnp.split(z, [COLS_A, COLS_A + COLS_B, COLS_A + COLS_B + COLS_C], axis=-1)
        ya = spatial_gating_branch(za, a_v_gain[l], a_w_s[l], a_b_s[l])
        yb = gqa_branch(zb, b_q_gain[l], b_k_gain[l], cos_b, sin_b)
        yc = mla_branch(zc, c_q_gain[l], c_w_uq[l], c_kv_gain[l], c_w_ukv[l], cos_c, sin_c)
        gates = jax.nn.sigmoid(zg.astype(jnp.float32)).astype(x.dtype).reshape(b, seq, N_BRANCH, d)
        merged = (gates[:, :, 0, :] * (ya @ w_pa[l])
                  + gates[:, :, 1, :] * (yb @ w_pb[l])
                  + gates[:, :, 2, :] * (yc @ w_pc[l]))
        x = x + g1[:, None, :] * (merged @ w_out[l])
        h2 = modulate(rms_norm(x, norm2_gain[l]), sh2, sc2)
        x = x + g2[:, None, :] * peer_ffn(h2, peer_w_query[l], peer_sub_keys[l], peer_u[l], peer_v[l])
    return rms_norm(x, final_gain)
```

```python
import functools

import numpy as np
import jax
import jax.numpy as jnp
from jax import lax
from jax.experimental import pallas as pl
from jax.experimental.pallas import tpu as pltpu

F32 = jnp.float32
BF16 = jnp.bfloat16

LANE = 128
VMEM_LIMIT = 56 * 1024 * 1024

D_MODEL = 1024
GRID_W = 64
BLOCK = 128
EPS = 1e-6
ROPE_THETA = 10000.0

A_GROUPS = 8
A_DIM = 64
A_WIDTH = A_GROUPS * A_DIM
B_HEADS = 8
B_KV_HEADS = 2
B_HEAD_DIM = 64
C_HEADS = 8
C_NOPE = 64
C_ROPE = 32
C_V = 64
C_Q_RANK = 256
C_KV_RANK = 128
COLS_A = 2 * A_WIDTH
COLS_B = (B_HEADS + 2 * B_KV_HEADS) * B_HEAD_DIM
COLS_C = C_Q_RANK + C_KV_RANK + C_ROPE
N_BRANCH = 3
PEER_HEADS = 8
PEER_N_KEYS = 128
PEER_HALF = 128
PEER_TOPK = 16

OFF_A = 0
OFF_B = OFF_A + COLS_A
OFF_C = OFF_B + COLS_B
OFF_G = OFF_C + C_Q_RANK + C_KV_RANK + LANE
W_IN_COLS = OFF_G + N_BRANCH * D_MODEL

CAND_PAIRS = [(a, b) for a in range(PEER_TOPK) for b in range(PEER_TOPK) if (a + 1) * (b + 1) <= PEER_TOPK]


def _cparams(sem):
    return pltpu.CompilerParams(dimension_semantics=sem, vmem_limit_bytes=VMEM_LIMIT)


def _const_spec(shape):
    nd = len(shape)
    return pl.BlockSpec(shape, lambda *_: (0,) * nd, pipeline_mode=pl.Buffered(1))


def _gelu(x):
    return 0.5 * x * (1.0 + jnp.tanh(0.7978845608028654 * (x + 0.044715 * (x * x * x))))


def _rms_rows(x, gain):
    return x * lax.rsqrt(jnp.mean(x * x, axis=-1, keepdims=True) + EPS) * gain


def _group_sumsq(x, ones_blockdiag):
    sq = x * x
    hi = sq.astype(BF16)
    lo = (sq - hi.astype(F32)).astype(BF16)
    return (jnp.dot(hi, ones_blockdiag, preferred_element_type=F32)
            + jnp.dot(lo, ones_blockdiag, preferred_element_type=F32))


def _swap_halves(x, group, lo_mask):
    w = x.shape[-1]
    half = group // 2
    return jnp.where(lo_mask, pltpu.roll(x, w - half, 1), pltpu.roll(x, half, 1))


def _ada_kernel(c_ref, w_ref, b_ref, o_ref):
    c = c_ref[...]
    ca = c * jax.nn.sigmoid(c)
    o_ref[0] = jnp.dot(ca, w_ref[0], preferred_element_type=F32,
                       precision=lax.Precision.HIGHEST) + b_ref[0]


def _ada_call(c_pad, w_ada, b_ada):
    depth, d, n = w_ada.shape
    tn = 1536
    return pl.pallas_call(
        _ada_kernel,
        grid=(depth, n // tn),
        in_specs=[
            pl.BlockSpec((8, d), lambda l, j: (0, 0)),
            pl.BlockSpec((1, d, tn), lambda l, j: (l, 0, j)),
            pl.BlockSpec((1, 1, tn), lambda l, j: (l, 0, j)),
        ],
        out_specs=pl.BlockSpec((1, 8, tn), lambda l, j: (l, 0, j)),
        out_shape=jax.ShapeDtypeStruct((depth, 8, n), F32),
        compiler_params=_cparams(("arbitrary", "arbitrary")),
        name="ada_mod",
    )(c_pad, w_ada, b_ada.reshape(depth, 1, n))


def _inproj_kernel(x_ref, mod_ref, n1g_ref, win_ref, avg_ref, ws_ref, bmat_ref, ones_ref,
                   qg_ref, kg_ref, cosb_ref, sinb_ref, cqg_ref, wuq_ref, ckvg_ref, wukv_ref,
                   cosc_ref, sinc_ref,
                   ya_ref, gates_ref, qb_ref, kb_ref, vtb_ref, qc_ref, kc_ref, vtc_ref):
    tm = x_ref.shape[0]
    d = D_MODEL
    shift = mod_ref[0, 0:1, :]
    scale = mod_ref[0, 1:2, :]
    h = (_rms_rows(x_ref[...], n1g_ref[...]) * (1.0 + scale) + shift).astype(BF16)

    za = jnp.dot(h, win_ref[:, OFF_A:OFF_A + COLS_A], preferred_element_type=F32)
    z = _gelu(za)
    u = z[:, :A_WIDTH]
    vn = _rms_rows(z[:, A_WIDTH:], avg_ref[...]).astype(BF16)
    grp = lax.broadcasted_iota(jnp.int32, (BLOCK, A_WIDTH), 1) // A_DIM
    for c in range(tm // BLOCK):
        vc = vn[c * BLOCK:(c + 1) * BLOCK]
        mixed = jnp.zeros((BLOCK, A_WIDTH), F32)
        for g in range(A_GROUPS):
            r = jnp.dot(ws_ref[g], vc, preferred_element_type=F32)
            mixed = jnp.where(grp == g, r, mixed)
        ya_ref[c * BLOCK:(c + 1) * BLOCK, :] = (
            u[c * BLOCK:(c + 1) * BLOCK] * (mixed + bmat_ref[...])).astype(BF16)

    zb = jnp.dot(h, win_ref[:, OFF_B:OFF_B + COLS_B], preferred_element_type=F32)
    qw = B_HEADS * B_HEAD_DIM
    kw = B_KV_HEADS * B_HEAD_DIM
    cosb = cosb_ref[...]
    sinb = sinb_ref[...]
    lane_q = lax.broadcasted_iota(jnp.int32, (tm, qw), 1)
    q = zb[:, :qw]
    qn = q * lax.rsqrt(_group_sumsq(q, ones_ref[...]) * (1.0 / B_HEAD_DIM) + EPS) * qg_ref[...]
    reps = qw // LANE
    qr = (qn * jnp.concatenate([cosb] * reps, axis=1)
          + _swap_halves(qn, B_HEAD_DIM, (lane_q % B_HEAD_DIM) < B_HEAD_DIM // 2)
          * jnp.concatenate([sinb] * reps, axis=1)) * (B_HEAD_DIM ** -0.5)
    lane_t = lax.broadcasted_iota(jnp.int32, (tm, LANE), 1)
    for t in range(reps):
        tile = qr[:, t * LANE:(t + 1) * LANE]
        qb_ref[:, t * LANE:(t + 1) * LANE] = jnp.where(lane_t < B_HEAD_DIM, tile, 0.0).astype(BF16)
        qb_ref[:, (reps + t) * LANE:(reps + t + 1) * LANE] = (
            jnp.where(lane_t >= B_HEAD_DIM, tile, 0.0).astype(BF16))
    k = zb[:, qw:qw + kw]
    kn = k * lax.rsqrt(_group_sumsq(k, ones_ref[0:kw, 0:kw]) * (1.0 / B_HEAD_DIM) + EPS) * kg_ref[...]
    kr = kn * cosb + _swap_halves(kn, B_HEAD_DIM, (lane_t % B_HEAD_DIM) < B_HEAD_DIM // 2) * sinb
    kb_ref[...] = kr.astype(BF16)
    vtb_ref[0] = zb[:, qw + kw:].T.astype(BF16)

    zc = jnp.dot(h, win_ref[:, OFF_C:OFF_G], preferred_element_type=F32)
    cosc = cosc_ref[...]
    sinc = sinc_ref[...]
    rope_lo = C_NOPE + C_ROPE // 2
    cqn = _rms_rows(zc[:, :C_Q_RANK], cqg_ref[...]).astype(BF16)
    qf = jnp.dot(cqn, wuq_ref[...], preferred_element_type=F32)
    lane_c = lax.broadcasted_iota(jnp.int32, qf.shape, 1)
    qrot = (qf * jnp.concatenate([cosc] * C_HEADS, axis=1)
            + _swap_halves(qf, C_ROPE, (lane_c % LANE) < rope_lo)
            * jnp.concatenate([sinc] * C_HEADS, axis=1)) * ((C_NOPE + C_ROPE) ** -0.5)
    qc_ref[...] = qrot.astype(BF16)
    ckvn = _rms_rows(zc[:, C_Q_RANK:C_Q_RANK + C_KV_RANK], ckvg_ref[...]).astype(BF16)
    kv = jnp.dot(ckvn, wukv_ref[...], preferred_element_type=F32)
    krope = zc[:, C_Q_RANK + C_KV_RANK:]
    krot = krope * cosc + _swap_halves(krope, C_ROPE, lane_t < rope_lo) * sinc
    kc_ref[...] = (kv[:, :C_HEADS * LANE] + jnp.concatenate([krot] * C_HEADS, axis=1)).astype(BF16)
    vtc_ref[0] = kv[:, C_HEADS * LANE:].T.astype(BF16)

    for n in range(N_BRANCH):
        zg = jnp.dot(h, win_ref[:, OFF_G + n * d:OFF_G + (n + 1) * d], preferred_element_type=F32)
        gates_ref[:, n * d:(n + 1) * d] = jax.nn.sigmoid(zg).astype(BF16)


def _inproj_call(x, mod, lw, tabs, tm, seq):
    t, d = x.shape
    nb = t // tm
    per_seq = seq // tm
    row = lambda i: (i, 0)
    pos = lambda i: (i % per_seq, 0)
    in_specs = [
        pl.BlockSpec((tm, d), row),
        pl.BlockSpec((1, 6, d), lambda i: (i // per_seq, 0, 0)),
        _const_spec((1, d)),
        _const_spec((d, W_IN_COLS)),
        _const_spec((1, A_WIDTH)),
        _const_spec((A_GROUPS, BLOCK, BLOCK)),
        _const_spec((BLOCK, A_WIDTH)),
        _const_spec((B_HEADS * B_HEAD_DIM, B_HEADS * B_HEAD_DIM)),
        _const_spec((1, B_HEADS * B_HEAD_DIM)),
        _const_spec((1, B_KV_HEADS * B_HEAD_DIM)),
        pl.BlockSpec((tm, LANE), pos),
        pl.BlockSpec((tm, LANE), pos),
        _const_spec((1, C_Q_RANK)),
        _const_spec((C_Q_RANK, C_HEADS * LANE)),
        _const_spec((1, C_KV_RANK)),
        _const_spec((C_KV_RANK, C_HEADS * (LANE + C_V))),
        pl.BlockSpec((tm, LANE), pos),
        pl.BlockSpec((tm, LANE), pos),
    ]
    out_shapes = (
        jax.ShapeDtypeStruct((t, A_WIDTH), BF16),
        jax.ShapeDtypeStruct((t, N_BRANCH * d), BF16),
        jax.ShapeDtypeStruct((t, B_HEADS * LANE), BF16),
        jax.ShapeDtypeStruct((t, LANE), BF16),
        jax.ShapeDtypeStruct((nb, B_KV_HEADS * B_HEAD_DIM, tm), BF16),
        jax.ShapeDtypeStruct((t, C_HEADS * LANE), BF16),
        jax.ShapeDtypeStruct((t, C_HEADS * LANE), BF16),
        jax.ShapeDtypeStruct((nb, C_HEADS * C_V, tm), BF16),
    )
    out_specs = (
        pl.BlockSpec((tm, A_WIDTH), row),
        pl.BlockSpec((tm, N_BRANCH * d), row),
        pl.BlockSpec((tm, B_HEADS * LANE), row),
        pl.BlockSpec((tm, LANE), row),
        pl.BlockSpec((1, B_KV_HEADS * B_HEAD_DIM, tm), lambda i: (i, 0, 0)),
        pl.BlockSpec((tm, C_HEADS * LANE), row),
        pl.BlockSpec((tm, C_HEADS * LANE), row),
        pl.BlockSpec((1, C_HEADS * C_V, tm), lambda i: (i, 0, 0)),
    )
    return pl.pallas_call(
        _inproj_kernel,
        grid=(nb,),
        in_specs=in_specs,
        out_specs=out_specs,
        out_shape=out_shapes,
        compiler_params=_cparams(("arbitrary",)),
        name="in_proj",
    )(x, mod, lw["n1g"], lw["w_in"], lw["a_v_gain"], lw["w_s"], lw["bmat"], tabs["ones"],
      lw["qg"], lw["kg"], tabs["cosb"], tabs["sinb"], lw["cqg"], lw["w_uq"], lw["ckvg"], lw["w_ukv"],
      tabs["cosc"], tabs["sinc"])


def _attn_kernel(q_ref, k_ref, vt_ref, o_ref, *, heads_share_kv, dv):
    tq = q_ref.shape[0]
    n_chunks, _, tk = vt_ref.shape
    ones = jnp.ones((16, tk), BF16)
    outs = []
    for hh in range(2):
        q = q_ref[:, hh * LANE:(hh + 1) * LANE]
        k_lane = 0 if heads_share_kv else hh * LANE
        v_row = 0 if heads_share_kv else hh * dv

        def body(c, carry, q=q, k_lane=k_lane, v_row=v_row):
            m, acc = carry
            start = pl.multiple_of(c * tk, tk)
            kc = k_ref[pl.ds(start, tk), k_lane:k_lane + LANE]
            s_t = lax.dot_general(kc, q, (((1,), (1,)), ((), ())), preferred_element_type=F32)
            m_new = jnp.maximum(m, jnp.max(s_t, axis=0, keepdims=True))
            p = jnp.exp(s_t - m_new).astype(BF16)
            alpha = jnp.exp(m - m_new)
            vt = jnp.concatenate([vt_ref[c, v_row:v_row + dv, :], ones], axis=0)
            acc = acc * alpha + jnp.dot(vt, p, preferred_element_type=F32)
            return m_new, acc

        m0 = jnp.full((1, tq), -jnp.inf, F32)
        acc0 = jnp.zeros((dv + 16, tq), F32)
        _, acc = lax.fori_loop(0, n_chunks, body, (m0, acc0))
        outs.append(acc[:dv] * (1.0 / acc[dv:dv + 1]))
    o_ref[...] = jnp.concatenate(outs, axis=0).T.astype(BF16)


def _attn_call(q, k, vt, batch, seq, tq, heads_share_kv, q_heads_per_kv):
    t = q.shape[0]
    n_heads = q.shape[1] // LANE
    tk = vt.shape[2]
    dv = 64
    n_chunks = seq // tk
    nq = seq // tq
    if heads_share_kv:
        k_spec = pl.BlockSpec((seq, LANE), lambda b, j, i: (b, 0))
        pairs_per_kv = q_heads_per_kv // 2
        vt_spec = pl.BlockSpec((n_chunks, dv, tk), lambda b, j, i: (b, j // pairs_per_kv, 0))
    else:
        k_spec = pl.BlockSpec((seq, 2 * LANE), lambda b, j, i: (b, j))
        vt_spec = pl.BlockSpec((n_chunks, 2 * dv, tk), lambda b, j, i: (b, j, 0))
    return pl.pallas_call(
        functools.partial(_attn_kernel, heads_share_kv=heads_share_kv, dv=dv),
        grid=(batch, n_heads // 2, nq),
        in_specs=[pl.BlockSpec((tq, 2 * LANE), lambda b, j, i: (b * nq + i, j)), k_spec, vt_spec],
        out_specs=pl.BlockSpec((tq, 2 * dv), lambda b, j, i: (b * nq + i, j)),
        out_shape=jax.ShapeDtypeStruct((t, n_heads * dv), BF16),
        compiler_params=_cparams(("arbitrary", "arbitrary", "arbitrary")),
        name="attn_shared_kv" if heads_share_kv else "attn_latent",
    )(q, k, vt)


def _merge_kernel(ya_ref, yb_ref, yc_ref, gates_ref, x_ref, mod_ref, n2g_ref, wpa_ref, wpb_ref,
                  wpc_ref, wout_ref, wq_ref, sk_ref,
                  x1_ref, h2_ref, s1_ref, s2_ref, e1_ref, e2_ref, tau_ref, vals_ref):
    d = D_MODEL
    merged = (gates_ref[:, 0:d].astype(F32) * jnp.dot(ya_ref[...], wpa_ref[...], preferred_element_type=F32)
              + gates_ref[:, d:2 * d].astype(F32) * jnp.dot(yb_ref[...], wpb_ref[...], preferred_element_type=F32)
              + gates_ref[:, 2 * d:3 * d].astype(F32) * jnp.dot(yc_ref[...], wpc_ref[...], preferred_element_type=F32))
    x1 = x_ref[...] + mod_ref[0, 2:3, :] * jnp.dot(merged.astype(BF16), wout_ref[...],
                                                   preferred_element_type=F32)
    x1_ref[...] = x1
    h2 = (_rms_rows(x1, n2g_ref[...]) * (1.0 + mod_ref[0, 4:5, :]) + mod_ref[0, 3:4, :]).astype(BF16)
    h2_ref[...] = h2
    qk = jnp.dot(h2, wq_ref[...], preferred_element_type=F32).astype(BF16)

    for h in range(PEER_HEADS):
        for p, s_ref in enumerate((s1_ref, s2_ref)):
            idx = 2 * h + p
            s = lax.dot_general(sk_ref[idx], qk[:, idx * PEER_HALF:(idx + 1) * PEER_HALF],
                                (((1,), (1,)), ((), ())), preferred_element_type=F32)
            s_ref[h] = s
            for a in range(PEER_TOPK):
                mx = jnp.max(s, axis=0, keepdims=True)
                vals_ref[p, a, h:h + 1, :] = mx
                s = jnp.where(s == mx, -jnp.inf, s)

    v1 = [vals_ref[0, a] for a in range(PEER_TOPK)]
    v2 = [vals_ref[1, a] for a in range(PEER_TOPK)]
    cands = [v1[a] + v2[b] for a, b in CAND_PAIRS]
    work = list(cands)
    tau = None
    for it in range(PEER_TOPK):
        mx = functools.reduce(jnp.maximum, work)
        if it == PEER_TOPK - 1:
            tau = mx
        else:
            work = [jnp.where(w == mx, -jnp.inf, w) for w in work]
    top = v1[0] + v2[0]
    zsum = functools.reduce(jnp.add, [jnp.where(c >= tau, jnp.exp(c - top), 0.0) for c in cands])
    rz = 1.0 / zsum
    tau_ref[...] = tau
    for h in range(PEER_HEADS):
        e1_ref[h] = jnp.exp(s1_ref[h] - v1[0][h:h + 1, :])
        e2_ref[h] = jnp.exp(s2_ref[h] - v2[0][h:h + 1, :]) * rz[h:h + 1, :]


def _merge_call(ya, yb, yc, gates, x, mod, lw, tm, seq):
    t, d = x.shape
    nb = t // tm
    per_seq = seq // tm
    row = lambda i: (i, 0)
    hk = pl.BlockSpec((PEER_HEADS, PEER_N_KEYS, tm), lambda i: (0, 0, i))
    hk_shape = jax.ShapeDtypeStruct((PEER_HEADS, PEER_N_KEYS, t), F32)
    return pl.pallas_call(
        _merge_kernel,
        grid=(nb,),
        in_specs=[
            pl.BlockSpec((tm, A_WIDTH), row),
            pl.BlockSpec((tm, A_WIDTH), row),
            pl.BlockSpec((tm, A_WIDTH), row),
            pl.BlockSpec((tm, N_BRANCH * d), row),
            pl.BlockSpec((tm, d), row),
            pl.BlockSpec((1, 6, d), lambda i: (i // per_seq, 0, 0)),
            _const_spec((1, d)),
            _const_spec((A_WIDTH, d)),
            _const_spec((A_WIDTH, d)),
            _const_spec((A_WIDTH, d)),
            _const_spec((d, d)),
            _const_spec((d, 2 * PEER_HEADS * PEER_HALF)),
            _const_spec((2 * PEER_HEADS, PEER_N_KEYS, PEER_HALF)),
        ],
        out_specs=(
            pl.BlockSpec((tm, d), row),
            pl.BlockSpec((tm, d), row),
            hk, hk, hk, hk,
            pl.BlockSpec((PEER_HEADS, tm), lambda i: (0, i)),
        ),
        out_shape=(
            jax.ShapeDtypeStruct((t, d), F32),
            jax.ShapeDtypeStruct((t, d), BF16),
            hk_shape, hk_shape, hk_shape, hk_shape,
            jax.ShapeDtypeStruct((PEER_HEADS, t), F32),
        ),
        scratch_shapes=[pltpu.VMEM((2, PEER_TOPK, PEER_HEADS, tm), F32)],
        compiler_params=_cparams(("arbitrary",)),
        name="merge_route",
    )(ya, yb, yc, gates, x, mod, lw["n2g"], lw["w_pa"], lw["w_pb"], lw["w_pc"], lw["w_out"],
      lw["w_query"], lw["sub_keys"])


def _expert_kernel(h2_ref, s1_ref, e1_ref, s2_ref, e2_ref, tau_ref, u_ref, vt_ref, x1_ref, mod_ref,
                   fg_ref, o_ref, acc_ref, w_ref, *, final_norm):
    ch = pl.program_id(1)
    rows = s1_ref.shape[1]

    @pl.when(ch == 0)
    def _():
        acc_ref[...] = jnp.zeros_like(acc_ref)

    act = lax.dot_general(u_ref[...], h2_ref[...], (((1,), (1,)), ((), ())), preferred_element_type=F32)
    for r in range(rows):
        gate = None
        for h in range(PEER_HEADS):
            pair = s2_ref[h] + s1_ref[h, r:r + 1, :]
            g = jnp.where(pair >= tau_ref[h:h + 1, :], e2_ref[h], 0.0) * e1_ref[h, r:r + 1, :]
            gate = g if gate is None else gate + g
        w_ref[r * PEER_N_KEYS:(r + 1) * PEER_N_KEYS, :] = (
            gate * _gelu(act[r * PEER_N_KEYS:(r + 1) * PEER_N_KEYS, :])).astype(BF16)
    acc_ref[...] += jnp.dot(vt_ref[...], w_ref[...], preferred_element_type=F32)

    @pl.when(ch == pl.num_programs(1) - 1)
    def _():
        x2 = x1_ref[...] + mod_ref[0, 5:6, :] * acc_ref[...].T
        if final_norm:
            x2 = _rms_rows(x2, fg_ref[...])
        o_ref[...] = x2


def _expert_call(h2, s1, e1, s2, e2, tau, u, vt, x1, mod, final_gain, tm, ec, seq, final_norm):
    t, d = x1.shape
    n_exp = u.shape[0]
    rows = ec // PEER_N_KEYS
    per_seq = seq // tm
    tok = lambda i, c: (i, 0)
    chunk_rows = pl.BlockSpec((PEER_HEADS, rows, tm), lambda i, c: (0, c, i))
    all_rows = pl.BlockSpec((PEER_HEADS, PEER_N_KEYS, tm), lambda i, c: (0, 0, i))
    return pl.pallas_call(
        functools.partial(_expert_kernel, final_norm=final_norm),
        grid=(t // tm, n_exp // ec),
        in_specs=[
            pl.BlockSpec((tm, d), tok),
            chunk_rows, chunk_rows, all_rows, all_rows,
            pl.BlockSpec((PEER_HEADS, tm), lambda i, c: (0, i)),
            pl.BlockSpec((ec, d), lambda i, c: (c, 0)),
            pl.BlockSpec((d, ec), lambda i, c: (0, c)),
            pl.BlockSpec((tm, d), tok),
            pl.BlockSpec((1, 6, d), lambda i, c: (i // per_seq, 0, 0)),
            pl.BlockSpec((1, d), lambda i, c: (0, 0)),
        ],
        out_specs=pl.BlockSpec((tm, d), tok),
        out_shape=jax.ShapeDtypeStruct((t, d), F32),
        scratch_shapes=[pltpu.VMEM((d, tm), F32), pltpu.VMEM((ec, tm), BF16)],
        compiler_params=_cparams(("arbitrary", "arbitrary")),
        name="peer_dense",
    )(h2, s1, e1, s2, e2, tau, u, vt, x1, mod, final_gain)


def _w_in_column_map():
    src = np.full((W_IN_COLS,), -1, np.int64)
    src[OFF_A:OFF_A + COLS_A] = np.arange(COLS_A)
    half = B_HEAD_DIM // 2
    pair = np.concatenate([np.arange(0, B_HEAD_DIM, 2), np.arange(1, B_HEAD_DIM, 2)])
    tiles = B_HEADS * B_HEAD_DIM // LANE
    for t in range(tiles):
        for side, head in enumerate((t, tiles + t)):
            dst = OFF_B + t * LANE + side * B_HEAD_DIM
            src[dst:dst + B_HEAD_DIM] = COLS_A + head * B_HEAD_DIM + pair
    qw = B_HEADS * B_HEAD_DIM
    for kv in range(B_KV_HEADS):
        dst = OFF_B + qw + kv * B_HEAD_DIM
        src[dst:dst + B_HEAD_DIM] = COLS_A + qw + kv * B_HEAD_DIM + pair
    kw = B_KV_HEADS * B_HEAD_DIM
    src[OFF_B + qw + kw:OFF_B + qw + 2 * kw] = COLS_A + qw + kw + np.arange(kw)
    base_c = COLS_A + COLS_B
    src[OFF_C:OFF_C + C_Q_RANK + C_KV_RANK] = base_c + np.arange(C_Q_RANK + C_KV_RANK)
    rope_src = base_c + C_Q_RANK + C_KV_RANK
    dst = OFF_C + C_Q_RANK + C_KV_RANK + C_NOPE
    src[dst:dst + C_ROPE // 2] = rope_src + np.arange(0, C_ROPE, 2)
    src[dst + C_ROPE // 2:dst + C_ROPE] = rope_src + np.arange(1, C_ROPE, 2)
    src[OFF_G:] = base_c + COLS_C + np.arange(N_BRANCH * D_MODEL)
    del half
    return src


def _take_cols(w, src):
    mask = jnp.asarray(src >= 0, w.dtype)
    return jnp.take(w, jnp.asarray(np.maximum(src, 0)), axis=-1) * mask


def _uq_column_map():
    src = np.full((C_HEADS * LANE,), -1, np.int64)
    per = C_NOPE + C_ROPE
    for h in range(C_HEADS):
        src[h * LANE:h * LANE + C_NOPE] = h * per + np.arange(C_NOPE)
        src[h * LANE + C_NOPE:h * LANE + C_NOPE + C_ROPE // 2] = h * per + C_NOPE + np.arange(0, C_ROPE, 2)
        src[h * LANE + C_NOPE + C_ROPE // 2:h * LANE + per] = h * per + C_NOPE + np.arange(1, C_ROPE, 2)
    return src


def _ukv_column_map():
    src = np.full((C_HEADS * (LANE + C_V),), -1, np.int64)
    per = C_NOPE + C_V
    for h in range(C_HEADS):
        src[h * LANE:h * LANE + C_NOPE] = h * per + np.arange(C_NOPE)
        src[C_HEADS * LANE + h * C_V:C_HEADS * LANE + (h + 1) * C_V] = h * per + C_NOPE + np.arange(C_V)
    return src


def _rope_tables(seq):
    rows = seq // GRID_W
    row = jnp.repeat(jnp.arange(rows, dtype=F32), GRID_W)
    col = jnp.tile(jnp.arange(GRID_W, dtype=F32), rows)

    def cos_sin(d_rot):
        half = d_rot // 2
        freq = ROPE_THETA ** (-jnp.arange(0, half, 2, dtype=F32) / half)
        ang = jnp.concatenate([row[:, None] * freq, col[:, None] * freq], axis=-1)
        return jnp.cos(ang), jnp.sin(ang)

    cb, sb = cos_sin(B_HEAD_DIM)
    cosb = jnp.tile(jnp.concatenate([cb, cb], axis=1), (1, LANE // B_HEAD_DIM))
    sinb = jnp.tile(jnp.concatenate([-sb, sb], axis=1), (1, LANE // B_HEAD_DIM))
    cc, sc = cos_sin(C_ROPE)
    one = jnp.ones((seq, C_NOPE), F32)
    pad = LANE - C_NOPE - C_ROPE
    cosc = jnp.concatenate([one, cc, cc, jnp.ones((seq, pad), F32)], axis=1)
    sinc = jnp.concatenate([0.0 * one, -sc, sc, jnp.zeros((seq, pad), F32)], axis=1)
    return cosb, sinb, cosc, sinc


def kernel(x, c, w_ada, b_ada, norm1_gain, w_in, a_v_gain, a_w_s, a_b_s, b_q_gain, b_k_gain,
           c_q_gain, c_w_uq, c_kv_gain, c_w_ukv, w_pa, w_pb, w_pc, w_out, norm2_gain,
           peer_w_query, peer_sub_keys, peer_u, peer_v, final_gain):
    batch, seq, d = x.shape
    depth = w_ada.shape[0]
    t = batch * seq
    assert d == D_MODEL and seq % GRID_W == 0
    tm = min(512, seq)
    tq = min(256, seq)
    tm_merge = min(256, seq)
    ec = 1024
    assert seq % tm == 0 and seq % tq == 0

    cosb, sinb, cosc, sinc = _rope_tables(seq)
    qw = B_HEADS * B_HEAD_DIM
    ones = jnp.asarray(np.kron(np.eye(qw // B_HEAD_DIM), np.ones((B_HEAD_DIM, B_HEAD_DIM))), BF16)
    tabs = dict(cosb=cosb, sinb=sinb, cosc=cosc, sinc=sinc, ones=ones)

    c_pad = jnp.zeros((8, d), F32).at[:batch].set(c)
    mod_all = _ada_call(c_pad, w_ada, b_ada)[:, :batch, :].reshape(depth, batch, 6, d)

    in_map, uq_map, ukv_map = _w_in_column_map(), _uq_column_map(), _ukv_column_map()
    pair = np.concatenate([np.arange(0, B_HEAD_DIM, 2), np.arange(1, B_HEAD_DIM, 2)])
    w_in_r = _take_cols(w_in, in_map).astype(BF16)
    w_uq_r = _take_cols(c_w_uq, uq_map).astype(BF16)
    w_ukv_r = _take_cols(c_w_ukv, ukv_map).astype(BF16)
    u_b = peer_u.astype(BF16)
    vt_b = jnp.swapaxes(peer_v, 1, 2).astype(BF16)

    xf = x.reshape(t, d)
    for l in range(depth):
        lw = dict(
            n1g=norm1_gain[l][None, :],
            w_in=w_in_r[l],
            a_v_gain=a_v_gain[l][None, :],
            w_s=a_w_s[l].astype(BF16),
            bmat=jnp.repeat(a_b_s[l].T, A_DIM, axis=1),
            qg=jnp.tile(b_q_gain[l][pair], B_HEADS)[None, :],
            kg=jnp.tile(b_k_gain[l][pair], B_KV_HEADS)[None, :],
            cqg=c_q_gain[l][None, :],
            w_uq=w_uq_r[l],
            ckvg=c_kv_gain[l][None, :],
            w_ukv=w_ukv_r[l],
            n2g=norm2_gain[l][None, :],
            w_pa=w_pa[l].astype(BF16),
            w_pb=w_pb[l].astype(BF16),
            w_pc=w_pc[l].astype(BF16),
            w_out=w_out[l].astype(BF16),
            w_query=peer_w_query[l].astype(BF16),
            sub_keys=peer_sub_keys[l].reshape(2 * PEER_HEADS, PEER_N_KEYS, PEER_HALF).astype(BF16),
        )
        mod = mod_all[l]
        ya, gates, qb, kb, vtb, qc, kc, vtc = _inproj_call(xf, mod, lw, tabs, tm, seq)
        yb = _attn_call(qb, kb, vtb, batch, seq, tq, True, B_HEADS // B_KV_HEADS)
        yc = _attn_call(qc, kc, vtc, batch, seq, tq, False, 1)
        x1, h2, s1, s2, e1, e2, tau = _merge_call(ya, yb, yc, gates, xf, mod, lw, tm_merge, seq)
        xf = _expert_call(h2, s1, e1, s2, e2, tau, u_b[l], vt_b[l], x1, mod, final_gain[None, :],
                          tm, ec, seq, l == depth - 1)
    return xf.reshape(batch, seq, d)
```

```python
import functools

import numpy as np
import jax
import jax.numpy as jnp
from jax import lax
from jax.experimental import pallas as pl
from jax.experimental.pallas import tpu as pltpu

F32 = jnp.float32
BF16 = jnp.bfloat16

LANE = 128
VMEM_LIMIT = 56 * 1024 * 1024

D_MODEL = 1024
GRID_W = 64
BLOCK = 128
EPS = 1e-6
ROPE_THETA = 10000.0
LOG2E = 1.4426950408889634

A_GROUPS = 8
A_DIM = 64
A_WIDTH = A_GROUPS * A_DIM
B_HEADS = 8
B_KV_HEADS = 2
B_HEAD_DIM = 64
C_HEADS = 8
C_NOPE = 64
C_ROPE = 32
C_V = 64
C_Q_RANK = 256
C_KV_RANK = 128
COLS_A = 2 * A_WIDTH
COLS_B = (B_HEADS + 2 * B_KV_HEADS) * B_HEAD_DIM
COLS_C = C_Q_RANK + C_KV_RANK + C_ROPE
N_BRANCH = 3
PEER_HEADS = 8
PEER_N_KEYS = 128
PEER_HALF = 128
PEER_TOPK = 16

OFF_A = 0
OFF_B = OFF_A + COLS_A
OFF_C = OFF_B + COLS_B
OFF_G = OFF_C + C_Q_RANK + C_KV_RANK + LANE
W_IN_COLS = OFF_G + N_BRANCH * D_MODEL

CAND_PAIRS = [(a, b) for a in range(PEER_TOPK) for b in range(PEER_TOPK) if (a + 1) * (b + 1) <= PEER_TOPK]


def _cparams(sem):
    return pltpu.CompilerParams(dimension_semantics=sem, vmem_limit_bytes=VMEM_LIMIT)


def _const_spec(shape):
    nd = len(shape)
    return pl.BlockSpec(shape, lambda *_: (0,) * nd, pipeline_mode=pl.Buffered(1))


def _gelu(x):
    return 0.5 * x * (1.0 + jnp.tanh(0.7978845608028654 * (x + 0.044715 * (x * x * x))))


def _rms_rows(x, gain):
    return x * lax.rsqrt(jnp.mean(x * x, axis=-1, keepdims=True) + EPS) * gain


def _group_sumsq(x, ones_blockdiag):
    sq = x * x
    hi = sq.astype(BF16)
    lo = (sq - hi.astype(F32)).astype(BF16)
    return (jnp.dot(hi, ones_blockdiag, preferred_element_type=F32)
            + jnp.dot(lo, ones_blockdiag, preferred_element_type=F32))


def _swap_halves(x, group, lo_mask):
    w = x.shape[-1]
    half = group // 2
    return jnp.where(lo_mask, pltpu.roll(x, w - half, 1), pltpu.roll(x, half, 1))


def _ada_kernel(c_ref, w_ref, b_ref, o_ref):
    c = c_ref[...]
    ca = c * jax.nn.sigmoid(c)
    o_ref[0] = jnp.dot(ca, w_ref[0], preferred_element_type=F32,
                       precision=lax.Precision.HIGHEST) + b_ref[0]


def _ada_call(c_pad, w_ada, b_ada):
    depth, d, n = w_ada.shape
    tn = 1536
    return pl.pallas_call(
        _ada_kernel,
        grid=(depth, n // tn),
        in_specs=[
            pl.BlockSpec((8, d), lambda l, j: (0, 0)),
            pl.BlockSpec((1, d, tn), lambda l, j: (l, 0, j)),
            pl.BlockSpec((1, 1, tn), lambda l, j: (l, 0, j)),
        ],
        out_specs=pl.BlockSpec((1, 8, tn), lambda l, j: (l, 0, j)),
        out_shape=jax.ShapeDtypeStruct((depth, 8, n), F32),
        compiler_params=_cparams(("arbitrary", "arbitrary")),
        name="ada_mod",
    )(c_pad, w_ada, b_ada.reshape(depth, 1, n))


def _inproj_kernel(x_ref, mod_ref, n1g_ref, win_ref, avg_ref, ws_ref, bmat_ref, ones_ref,
                   qg_ref, kg_ref, cosb_ref, sinb_ref, cqg_ref, wuq_ref, ckvg_ref, wukv_ref,
                   cosc_ref, sinc_ref,
                   ya_ref, gates_ref, qb_ref, kb_ref, vtb_ref, qc_ref, kc_ref, vtc_ref):
    tm = x_ref.shape[0]
    d = D_MODEL
    shift = mod_ref[0, 0:1, :]
    scale = mod_ref[0, 1:2, :]
    h = (_rms_rows(x_ref[...], n1g_ref[...]) * (1.0 + scale) + shift).astype(BF16)

    za = jnp.dot(h, win_ref[:, OFF_A:OFF_A + COLS_A], preferred_element_type=F32)
    z = _gelu(za)
    u = z[:, :A_WIDTH]
    vn = _rms_rows(z[:, A_WIDTH:], avg_ref[...]).astype(BF16)
    grp = lax.broadcasted_iota(jnp.int32, (BLOCK, A_WIDTH), 1) // A_DIM
    for c in range(tm // BLOCK):
        vc = vn[c * BLOCK:(c + 1) * BLOCK]
        mixed = jnp.zeros((BLOCK, A_WIDTH), F32)
        for g in range(A_GROUPS):
            r = jnp.dot(ws_ref[g], vc, preferred_element_type=F32)
            mixed = jnp.where(grp == g, r, mixed)
        ya_ref[c * BLOCK:(c + 1) * BLOCK, :] = (
            u[c * BLOCK:(c + 1) * BLOCK] * (mixed + bmat_ref[...])).astype(BF16)

    zb = jnp.dot(h, win_ref[:, OFF_B:OFF_B + COLS_B], preferred_element_type=F32)
    qw = B_HEADS * B_HEAD_DIM
    kw = B_KV_HEADS * B_HEAD_DIM
    cosb = cosb_ref[...]
    sinb = sinb_ref[...]
    lane_q = lax.broadcasted_iota(jnp.int32, (tm, qw), 1)
    q = zb[:, :qw]
    qn = q * lax.rsqrt(_group_sumsq(q, ones_ref[...]) * (1.0 / B_HEAD_DIM) + EPS) * qg_ref[...]
    reps = qw // LANE
    qr = (qn * jnp.concatenate([cosb] * reps, axis=1)
          + _swap_halves(qn, B_HEAD_DIM, (lane_q % B_HEAD_DIM) < B_HEAD_DIM // 2)
          * jnp.concatenate([sinb] * reps, axis=1)) * (B_HEAD_DIM ** -0.5 * LOG2E)
    lane_t = lax.broadcasted_iota(jnp.int32, (tm, LANE), 1)
    for t in range(reps):
        tile = qr[:, t * LANE:(t + 1) * LANE]
        qb_ref[:, t * LANE:(t + 1) * LANE] = jnp.where(lane_t < B_HEAD_DIM, tile, 0.0).astype(BF16)
        qb_ref[:, (reps + t) * LANE:(reps + t + 1) * LANE] = (
            jnp.where(lane_t >= B_HEAD_DIM, tile, 0.0).astype(BF16))
    k = zb[:, qw:qw + kw]
    kn = k * lax.rsqrt(_group_sumsq(k, ones_ref[0:kw, 0:kw]) * (1.0 / B_HEAD_DIM) + EPS) * kg_ref[...]
    kr = kn * cosb + _swap_halves(kn, B_HEAD_DIM, (lane_t % B_HEAD_DIM) < B_HEAD_DIM // 2) * sinb
    kb_ref[...] = kr.astype(BF16)
    vtb_ref[0] = zb[:, qw + kw:].T.astype(BF16)

    zc = jnp.dot(h, win_ref[:, OFF_C:OFF_G], preferred_element_type=F32)
    cosc = cosc_ref[...]
    sinc = sinc_ref[...]
    rope_lo = C_NOPE + C_ROPE // 2
    cqn = _rms_rows(zc[:, :C_Q_RANK], cqg_ref[...]).astype(BF16)
    qf = jnp.dot(cqn, wuq_ref[...], preferred_element_type=F32)
    lane_c = lax.broadcasted_iota(jnp.int32, qf.shape, 1)
    qrot = (qf * jnp.concatenate([cosc] * C_HEADS, axis=1)
            + _swap_halves(qf, C_ROPE, (lane_c % LANE) < rope_lo)
            * jnp.concatenate([sinc] * C_HEADS, axis=1)) * ((C_NOPE + C_ROPE) ** -0.5 * LOG2E)
    qc_ref[...] = qrot.astype(BF16)
    ckvn = _rms_rows(zc[:, C_Q_RANK:C_Q_RANK + C_KV_RANK], ckvg_ref[...]).astype(BF16)
    kv = jnp.dot(ckvn, wukv_ref[...], preferred_element_type=F32)
    krope = zc[:, C_Q_RANK + C_KV_RANK:]
    krot = krope * cosc + _swap_halves(krope, C_ROPE, lane_t < rope_lo) * sinc
    kc_ref[...] = (kv[:, :C_HEADS * LANE] + jnp.concatenate([krot] * C_HEADS, axis=1)).astype(BF16)
    vtc_ref[0] = kv[:, C_HEADS * LANE:].T.astype(BF16)

    for n in range(N_BRANCH):
        zg = jnp.dot(h, win_ref[:, OFF_G + n * d:OFF_G + (n + 1) * d], preferred_element_type=F32)
        gates_ref[:, n * d:(n + 1) * d] = jax.nn.sigmoid(zg).astype(BF16)


def _inproj_call(x, mod, lw, tabs, tm, seq):
    t, d = x.shape
    nb = t // tm
    per_seq = seq // tm
    row = lambda i: (i, 0)
    pos = lambda i: (i % per_seq, 0)
    in_specs = [
        pl.BlockSpec((tm, d), row),
        pl.BlockSpec((1, 6, d), lambda i: (i // per_seq, 0, 0)),
        _const_spec((1, d)),
        _const_spec((d, W_IN_COLS)),
        _const_spec((1, A_WIDTH)),
        _const_spec((A_GROUPS, BLOCK, BLOCK)),
        _const_spec((BLOCK, A_WIDTH)),
        _const_spec((B_HEADS * B_HEAD_DIM, B_HEADS * B_HEAD_DIM)),
        _const_spec((1, B_HEADS * B_HEAD_DIM)),
        _const_spec((1, B_KV_HEADS * B_HEAD_DIM)),
        pl.BlockSpec((tm, LANE), pos),
        pl.BlockSpec((tm, LANE), pos),
        _const_spec((1, C_Q_RANK)),
        _const_spec((C_Q_RANK, C_HEADS * LANE)),
        _const_spec((1, C_KV_RANK)),
        _const_spec((C_KV_RANK, C_HEADS * (LANE + C_V))),
        pl.BlockSpec((tm, LANE), pos),
        pl.BlockSpec((tm, LANE), pos),
    ]
    out_shapes = (
        jax.ShapeDtypeStruct((t, A_WIDTH), BF16),
        jax.ShapeDtypeStruct((t, N_BRANCH * d), BF16),
        jax.ShapeDtypeStruct((t, B_HEADS * LANE), BF16),
        jax.ShapeDtypeStruct((t, LANE), BF16),
        jax.ShapeDtypeStruct((nb, B_KV_HEADS * B_HEAD_DIM, tm), BF16),
        jax.ShapeDtypeStruct((t, C_HEADS * LANE), BF16),
        jax.ShapeDtypeStruct((t, C_HEADS * LANE), BF16),
        jax.ShapeDtypeStruct((nb, C_HEADS * C_V, tm), BF16),
    )
    out_specs = (
        pl.BlockSpec((tm, A_WIDTH), row),
        pl.BlockSpec((tm, N_BRANCH * d), row),
        pl.BlockSpec((tm, B_HEADS * LANE), row),
        pl.BlockSpec((tm, LANE), row),
        pl.BlockSpec((1, B_KV_HEADS * B_HEAD_DIM, tm), lambda i: (i, 0, 0)),
        pl.BlockSpec((tm, C_HEADS * LANE), row),
        pl.BlockSpec((tm, C_HEADS * LANE), row),
        pl.BlockSpec((1, C_HEADS * C_V, tm), lambda i: (i, 0, 0)),
    )
    return pl.pallas_call(
        _inproj_kernel,
        grid=(nb,),
        in_specs=in_specs,
        out_specs=out_specs,
        out_shape=out_shapes,
        compiler_params=_cparams(("arbitrary",)),
        name="in_proj",
    )(x, mod, lw["n1g"], lw["w_in"], lw["a_v_gain"], lw["w_s"], lw["bmat"], tabs["ones"],
      lw["qg"], lw["kg"], tabs["cosb"], tabs["sinb"], lw["cqg"], lw["w_uq"], lw["ckvg"], lw["w_ukv"],
      tabs["cosc"], tabs["sinc"])


def _attn_kernel(q_ref, k_ref, vt_ref, o_ref, s_ref, *, heads_share_kv, dv):
    tq = q_ref.shape[0]
    n_chunks, _, tk = vt_ref.shape
    ones = jnp.ones((16, tk), BF16)
    hs = range(2)
    qs = [q_ref[:, hh * LANE:(hh + 1) * LANE] for hh in hs]

    def scores(c, slot):
        start = pl.multiple_of(c * tk, tk)
        s_t = [lax.dot_general(k_ref[pl.ds(start, tk), (0 if heads_share_kv else hh * LANE):
                                     (LANE if heads_share_kv else (hh + 1) * LANE)],
                               qs[hh], (((1,), (1,)), ((), ())), preferred_element_type=F32) for hh in hs]
        for hh in hs:
            s_ref[slot, hh] = s_t[hh]
        return tuple(jnp.max(s_t[hh], axis=0, keepdims=True) for hh in hs)

    def accumulate(c, slot, state, cmax):
        m_new = [jnp.maximum(state[hh][0], cmax[hh]) for hh in hs]
        p = [jnp.exp2(s_ref[slot, hh] - m_new[hh]).astype(BF16) for hh in hs]
        alpha = [jnp.exp2(state[hh][0] - m_new[hh]) for hh in hs]
        vt = [jnp.concatenate([vt_ref[c, (0 if heads_share_kv else hh * dv):
                                      (dv if heads_share_kv else (hh + 1) * dv), :], ones], axis=0) for hh in hs]
        pv = [jnp.dot(vt[hh], p[hh], preferred_element_type=F32) for hh in hs]
        return tuple((m_new[hh], state[hh][1] * alpha[hh] + pv[hh]) for hh in hs)

    def body(i, carry):
        state, cmax = carry
        c = 2 * i
        cmax1 = scores(c + 1, 1)
        state = accumulate(c, 0, state, cmax)
        cmax0 = scores(c + 2, 0)
        state = accumulate(c + 1, 1, state, cmax1)
        return state, cmax0

    state = tuple((jnp.full((1, tq), -jnp.inf, F32), jnp.zeros((dv + 16, tq), F32)) for _ in hs)
    cmax = scores(0, 0)
    if n_chunks > 1:
        assert n_chunks % 2 == 0
        state, cmax = lax.fori_loop(0, n_chunks // 2 - 1, body, (state, cmax))
        cmax1 = scores(n_chunks - 1, 1)
        state = accumulate(n_chunks - 2, 0, state, cmax)
        state = accumulate(n_chunks - 1, 1, state, cmax1)
    else:
        state = accumulate(0, 0, state, cmax)
    outs = [acc[:dv] * (1.0 / acc[dv:dv + 1]) for _, acc in state]
    o_ref[...] = jnp.concatenate(outs, axis=0).T.astype(BF16)


def _attn_call(q, k, vt, batch, seq, tq, heads_share_kv, q_heads_per_kv):
    t = q.shape[0]
    n_heads = q.shape[1] // LANE
    tk = vt.shape[2]
    dv = 64
    n_chunks = seq // tk
    nq = seq // tq
    if heads_share_kv:
        k_spec = pl.BlockSpec((seq, LANE), lambda b, j, i: (b, 0))
        pairs_per_kv = q_heads_per_kv // 2
        vt_spec = pl.BlockSpec((n_chunks, dv, tk), lambda b, j, i: (b, j // pairs_per_kv, 0))
    else:
        k_spec = pl.BlockSpec((seq, 2 * LANE), lambda b, j, i: (b, j))
        vt_spec = pl.BlockSpec((n_chunks, 2 * dv, tk), lambda b, j, i: (b, j, 0))
    return pl.pallas_call(
        functools.partial(_attn_kernel, heads_share_kv=heads_share_kv, dv=dv),
        grid=(batch, n_heads // 2, nq),
        in_specs=[pl.BlockSpec((tq, 2 * LANE), lambda b, j, i: (b * nq + i, j)), k_spec, vt_spec],
        out_specs=pl.BlockSpec((tq, 2 * dv), lambda b, j, i: (b * nq + i, j)),
        out_shape=jax.ShapeDtypeStruct((t, n_heads * dv), BF16),
        compiler_params=_cparams(("arbitrary", "arbitrary", "arbitrary")),
        scratch_shapes=[pltpu.VMEM((2, 2, tk, tq), F32)],
        name="attn_shared_kv" if heads_share_kv else "attn_latent",
    )(q, k, vt)


def _merge_kernel(ya_ref, yb_ref, yc_ref, gates_ref, x_ref, mod_ref, n2g_ref, wpa_ref, wpb_ref,
                  wpc_ref, wout_ref, wq_ref, sk_ref,
                  x1_ref, h2_ref, s1_ref, s2_ref, e1_ref, e2_ref, tau_ref, vals_ref):
    d = D_MODEL
    merged = (gates_ref[:, 0:d].astype(F32) * jnp.dot(ya_ref[...], wpa_ref[...], preferred_element_type=F32)
              + gates_ref[:, d:2 * d].astype(F32) * jnp.dot(yb_ref[...], wpb_ref[...], preferred_element_type=F32)
              + gates_ref[:, 2 * d:3 * d].astype(F32) * jnp.dot(yc_ref[...], wpc_ref[...], preferred_element_type=F32))
    x1 = x_ref[...] + mod_ref[0, 2:3, :] * jnp.dot(merged.astype(BF16), wout_ref[...],
                                                   preferred_element_type=F32)
    x1_ref[...] = x1
    h2 = (_rms_rows(x1, n2g_ref[...]) * (1.0 + mod_ref[0, 4:5, :]) + mod_ref[0, 3:4, :]).astype(BF16)
    h2_ref[...] = h2
    qk = jnp.dot(h2, wq_ref[...], preferred_element_type=F32).astype(BF16)

    for h in range(PEER_HEADS):
        for p, s_ref in enumerate((s1_ref, s2_ref)):
            idx = 2 * h + p
            s = lax.dot_general(sk_ref[idx], qk[:, idx * PEER_HALF:(idx + 1) * PEER_HALF],
                                (((1,), (1,)), ((), ())), preferred_element_type=F32)
            s_ref[h] = s
            for a in range(PEER_TOPK):
                mx = jnp.max(s, axis=0, keepdims=True)
                vals_ref[p, a, h:h + 1, :] = mx
                s = jnp.where(s == mx, -jnp.inf, s)

    v1 = [vals_ref[0, a] for a in range(PEER_TOPK)]
    v2 = [vals_ref[1, a] for a in range(PEER_TOPK)]
    cands = [v1[a] + v2[b] for a, b in CAND_PAIRS]
    work = list(cands)
    tau = None
    for it in range(PEER_TOPK):
        mx = functools.reduce(jnp.maximum, work)
        if it == PEER_TOPK - 1:
            tau = mx
        else:
            work = [jnp.where(w == mx, -jnp.inf, w) for w in work]
    top = v1[0] + v2[0]
    zsum = functools.reduce(jnp.add, [jnp.where(c >= tau, jnp.exp(c - top), 0.0) for c in cands])
    rz = 1.0 / zsum
    tau_ref[...] = tau
    for h in range(PEER_HEADS):
        e1_ref[h] = jnp.exp(s1_ref[h] - v1[0][h:h + 1, :])
        e2_ref[h] = jnp.exp(s2_ref[h] - v2[0][h:h + 1, :]) * rz[h:h + 1, :]


def _merge_call(ya, yb, yc, gates, x, mod, lw, tm, seq):
    t, d = x.shape
    nb = t // tm
    per_seq = seq // tm
    row = lambda i: (i, 0)
    hk = pl.BlockSpec((PEER_HEADS, PEER_N_KEYS, tm), lambda i: (0, 0, i))
    hk_shape = jax.ShapeDtypeStruct((PEER_HEADS, PEER_N_KEYS, t), F32)
    return pl.pallas_call(
        _merge_kernel,
        grid=(nb,),
        in_specs=[
            pl.BlockSpec((tm, A_WIDTH), row),
            pl.BlockSpec((tm, A_WIDTH), row),
            pl.BlockSpec((tm, A_WIDTH), row),
            pl.BlockSpec((tm, N_BRANCH * d), row),
            pl.BlockSpec((tm, d), row),
            pl.BlockSpec((1, 6, d), lambda i: (i // per_seq, 0, 0)),
            _const_spec((1, d)),
            _const_spec((A_WIDTH, d)),
            _const_spec((A_WIDTH, d)),
            _const_spec((A_WIDTH, d)),
            _const_spec((d, d)),
            _const_spec((d, 2 * PEER_HEADS * PEER_HALF)),
            _const_spec((2 * PEER_HEADS, PEER_N_KEYS, PEER_HALF)),
        ],
        out_specs=(
            pl.BlockSpec((tm, d), row),
            pl.BlockSpec((tm, d), row),
            hk, hk, hk, hk,
            pl.BlockSpec((PEER_HEADS, tm), lambda i: (0, i)),
        ),
        out_shape=(
            jax.ShapeDtypeStruct((t, d), F32),
            jax.ShapeDtypeStruct((t, d), BF16),
            hk_shape, hk_shape, hk_shape, hk_shape,
            jax.ShapeDtypeStruct((PEER_HEADS, t), F32),
        ),
        scratch_shapes=[pltpu.VMEM((2, PEER_TOPK, PEER_HEADS, tm), F32)],
        compiler_params=_cparams(("arbitrary",)),
        name="merge_route",
    )(ya, yb, yc, gates, x, mod, lw["n2g"], lw["w_pa"], lw["w_pb"], lw["w_pc"], lw["w_out"],
      lw["w_query"], lw["sub_keys"])


def _expert_kernel(h2_ref, s1_ref, e1_ref, s2_ref, e2_ref, tau_ref, u_ref, vt_ref, x1_ref, mod_ref,
                   fg_ref, o_ref, acc_ref, w_ref, *, final_norm):
    ch = pl.program_id(1)
    rows = s1_ref.shape[1]

    @pl.when(ch == 0)
    def _():
        acc_ref[...] = jnp.zeros_like(acc_ref)

    act = lax.dot_general(u_ref[...], h2_ref[...], (((1,), (1,)), ((), ())), preferred_element_type=F32)
    for r in range(rows):
        gate = None
        for h in range(PEER_HEADS):
            pair = s2_ref[h] + s1_ref[h, r:r + 1, :]
            g = jnp.where(pair >= tau_ref[h:h + 1, :], e2_ref[h], 0.0) * e1_ref[h, r:r + 1, :]
            gate = g if gate is None else gate + g
        w_ref[r * PEER_N_KEYS:(r + 1) * PEER_N_KEYS, :] = (
            gate * _gelu(act[r * PEER_N_KEYS:(r + 1) * PEER_N_KEYS, :])).astype(BF16)
    acc_ref[...] += jnp.dot(vt_ref[...], w_ref[...], preferred_element_type=F32)

    @pl.when(ch == pl.num_programs(1) - 1)
    def _():
        x2 = x1_ref[...] + mod_ref[0, 5:6, :] * acc_ref[...].T
        if final_norm:
            x2 = _rms_rows(x2, fg_ref[...])
        o_ref[...] = x2


def _expert_call(h2, s1, e1, s2, e2, tau, u, vt, x1, mod, final_gain, tm, ec, seq, final_norm):
    t, d = x1.shape
    n_exp = u.shape[0]
    rows = ec // PEER_N_KEYS
    per_seq = seq // tm
    tok = lambda i, c: (i, 0)
    chunk_rows = pl.BlockSpec((PEER_HEADS, rows, tm), lambda i, c: (0, c, i))
    all_rows = pl.BlockSpec((PEER_HEADS, PEER_N_KEYS, tm), lambda i, c: (0, 0, i))
    return pl.pallas_call(
        functools.partial(_expert_kernel, final_norm=final_norm),
        grid=(t // tm, n_exp // ec),
        in_specs=[
            pl.BlockSpec((tm, d), tok),
            chunk_rows, chunk_rows, all_rows, all_rows,
            pl.BlockSpec((PEER_HEADS, tm), lambda i, c: (0, i)),
            pl.BlockSpec((ec, d), lambda i, c: (c, 0)),
            pl.BlockSpec((d, ec), lambda i, c: (0, c)),
            pl.BlockSpec((tm, d), tok),
            pl.BlockSpec((1, 6, d), lambda i, c: (i // per_seq, 0, 0)),
            pl.BlockSpec((1, d), lambda i, c: (0, 0)),
        ],
        out_specs=pl.BlockSpec((tm, d), tok),
        out_shape=jax.ShapeDtypeStruct((t, d), F32),
        scratch_shapes=[pltpu.VMEM((d, tm), F32), pltpu.VMEM((ec, tm), BF16)],
        compiler_params=_cparams(("arbitrary", "arbitrary")),
        name="peer_dense",
    )(h2, s1, e1, s2, e2, tau, u, vt, x1, mod, final_gain)


def _w_in_column_map():
    src = np.full((W_IN_COLS,), -1, np.int64)
    src[OFF_A:OFF_A + COLS_A] = np.arange(COLS_A)
    half = B_HEAD_DIM // 2
    pair = np.concatenate([np.arange(0, B_HEAD_DIM, 2), np.arange(1, B_HEAD_DIM, 2)])
    tiles = B_HEADS * B_HEAD_DIM // LANE
    for t in range(tiles):
        for side, head in enumerate((t, tiles + t)):
            dst = OFF_B + t * LANE + side * B_HEAD_DIM
            src[dst:dst + B_HEAD_DIM] = COLS_A + head * B_HEAD_DIM + pair
    qw = B_HEADS * B_HEAD_DIM
    for kv in range(B_KV_HEADS):
        dst = OFF_B + qw + kv * B_HEAD_DIM
        src[dst:dst + B_HEAD_DIM] = COLS_A + qw + kv * B_HEAD_DIM + pair
    kw = B_KV_HEADS * B_HEAD_DIM
    src[OFF_B + qw + kw:OFF_B + qw + 2 * kw] = COLS_A + qw + kw + np.arange(kw)
    base_c = COLS_A + COLS_B
    src[OFF_C:OFF_C + C_Q_RANK + C_KV_RANK] = base_c + np.arange(C_Q_RANK + C_KV_RANK)
    rope_src = base_c + C_Q_RANK + C_KV_RANK
    dst = OFF_C + C_Q_RANK + C_KV_RANK + C_NOPE
    src[dst:dst + C_ROPE // 2] = rope_src + np.arange(0, C_ROPE, 2)
    src[dst + C_ROPE // 2:dst + C_ROPE] = rope_src + np.arange(1, C_ROPE, 2)
    src[OFF_G:] = base_c + COLS_C + np.arange(N_BRANCH * D_MODEL)
    del half
    return src


def _take_cols(w, src):
    mask = jnp.asarray(src >= 0, w.dtype)
    return jnp.take(w, jnp.asarray(np.maximum(src, 0)), axis=-1) * mask


def _uq_column_map():
    src = np.full((C_HEADS * LANE,), -1, np.int64)
    per = C_NOPE + C_ROPE
    for h in range(C_HEADS):
        src[h * LANE:h * LANE + C_NOPE] = h * per + np.arange(C_NOPE)
        src[h * LANE + C_NOPE:h * LANE + C_NOPE + C_ROPE // 2] = h * per + C_NOPE + np.arange(0, C_ROPE, 2)
        src[h * LANE + C_NOPE + C_ROPE // 2:h * LANE + per] = h * per + C_NOPE + np.arange(1, C_ROPE, 2)
    return src


def _ukv_column_map():
    src = np.full((C_HEADS * (LANE + C_V),), -1, np.int64)
    per = C_NOPE + C_V
    for h in range(C_HEADS):
        src[h * LANE:h * LANE + C_NOPE] = h * per + np.arange(C_NOPE)
        src[C_HEADS * LANE + h * C_V:C_HEADS * LANE + (h + 1) * C_V] = h * per + C_NOPE + np.arange(C_V)
    return src


def _rope_tables(seq):
    rows = seq // GRID_W
    row = jnp.repeat(jnp.arange(rows, dtype=F32), GRID_W)
    col = jnp.tile(jnp.arange(GRID_W, dtype=F32), rows)

    def cos_sin(d_rot):
        half = d_rot // 2
        freq = ROPE_THETA ** (-jnp.arange(0, half, 2, dtype=F32) / half)
        ang = jnp.concatenate([row[:, None] * freq, col[:, None] * freq], axis=-1)
        return jnp.cos(ang), jnp.sin(ang)

    cb, sb = cos_sin(B_HEAD_DIM)
    cosb = jnp.tile(jnp.concatenate([cb, cb], axis=1), (1, LANE // B_HEAD_DIM))
    sinb = jnp.tile(jnp.concatenate([-sb, sb], axis=1), (1, LANE // B_HEAD_DIM))
    cc, sc = cos_sin(C_ROPE)
    one = jnp.ones((seq, C_NOPE), F32)
    pad = LANE - C_NOPE - C_ROPE
    cosc = jnp.concatenate([one, cc, cc, jnp.ones((seq, pad), F32)], axis=1)
    sinc = jnp.concatenate([0.0 * one, -sc, sc, jnp.zeros((seq, pad), F32)], axis=1)
    return cosb, sinb, cosc, sinc


def kernel(x, c, w_ada, b_ada, norm1_gain, w_in, a_v_gain, a_w_s, a_b_s, b_q_gain, b_k_gain,
           c_q_gain, c_w_uq, c_kv_gain, c_w_ukv, w_pa, w_pb, w_pc, w_out, norm2_gain,
           peer_w_query, peer_sub_keys, peer_u, peer_v, final_gain):
    batch, seq, d = x.shape
    depth = w_ada.shape[0]
    t = batch * seq
    assert d == D_MODEL and seq % GRID_W == 0
    tm = min(512, seq)
    tq = min(256, seq)
    tm_merge = min(256, seq)
    ec = 1024
    assert seq % tm == 0 and seq % tq == 0

    cosb, sinb, cosc, sinc = _rope_tables(seq)
    qw = B_HEADS * B_HEAD_DIM
    ones = jnp.asarray(np.kron(np.eye(qw // B_HEAD_DIM), np.ones((B_HEAD_DIM, B_HEAD_DIM))), BF16)
    tabs = dict(cosb=cosb, sinb=sinb, cosc=cosc, sinc=sinc, ones=ones)

    c_pad = jnp.zeros((8, d), F32).at[:batch].set(c)
    mod_all = _ada_call(c_pad, w_ada, b_ada)[:, :batch, :].reshape(depth, batch, 6, d)

    in_map, uq_map, ukv_map = _w_in_column_map(), _uq_column_map(), _ukv_column_map()
    pair = np.concatenate([np.arange(0, B_HEAD_DIM, 2), np.arange(1, B_HEAD_DIM, 2)])
    w_in_r = _take_cols(w_in, in_map).astype(BF16)
    w_uq_r = _take_cols(c_w_uq, uq_map).astype(BF16)
    w_ukv_r = _take_cols(c_w_ukv, ukv_map).astype(BF16)
    u_b = peer_u.astype(BF16)
    vt_b = jnp.swapaxes(peer_v, 1, 2).astype(BF16)

    xf = x.reshape(t, d)
    for l in range(depth):
        lw = dict(
            n1g=norm1_gain[l][None, :],
            w_in=w_in_r[l],
            a_v_gain=a_v_gain[l][None, :],
            w_s=a_w_s[l].astype(BF16),
            bmat=jnp.repeat(a_b_s[l].T, A_DIM, axis=1),
            qg=jnp.tile(b_q_gain[l][pair], B_HEADS)[None, :],
            kg=jnp.tile(b_k_gain[l][pair], B_KV_HEADS)[None, :],
            cqg=c_q_gain[l][None, :],
            w_uq=w_uq_r[l],
            ckvg=c_kv_gain[l][None, :],
            w_ukv=w_ukv_r[l],
            n2g=norm2_gain[l][None, :],
            w_pa=w_pa[l].astype(BF16),
            w_pb=w_pb[l].astype(BF16),
            w_pc=w_pc[l].astype(BF16),
            w_out=w_out[l].astype(BF16),
            w_query=peer_w_query[l].astype(BF16),
            sub_keys=peer_sub_keys[l].reshape(2 * PEER_HEADS, PEER_N_KEYS, PEER_HALF).astype(BF16),
        )
        mod = mod_all[l]
        ya, gates, qb, kb, vtb, qc, kc, vtc = _inproj_call(xf, mod, lw, tabs, tm, seq)
        yb = _attn_call(qb, kb, vtb, batch, seq, tq, True, B_HEADS // B_KV_HEADS)
        yc = _attn_call(qc, kc, vtc, batch, seq, tq, False, 1)
        x1, h2, s1, s2, e1, e2, tau = _merge_call(ya, yb, yc, gates, xf, mod, lw, tm_merge, seq)
        xf = _expert_call(h2, s1, e1, s2, e2, tau, u_b[l], vt_b[l], x1, mod, final_gain[None, :],
                          tm, ec, seq, l == depth - 1)
    return xf.reshape(batch, seq, d)
```

```python
import functools

import numpy as np
import jax
import jax.numpy as jnp
from jax import lax
from jax.experimental import pallas as pl
from jax.experimental.pallas import tpu as pltpu

F32 = jnp.float32
BF16 = jnp.bfloat16

LANE = 128
VMEM_LIMIT = 56 * 1024 * 1024

D_MODEL = 1024
GRID_W = 64
BLOCK = 128
EPS = 1e-6
ROPE_THETA = 10000.0
LOG2E = 1.4426950408889634

A_GROUPS = 8
A_DIM = 64
A_WIDTH = A_GROUPS * A_DIM
B_HEADS = 8
B_KV_HEADS = 2
B_HEAD_DIM = 64
C_HEADS = 8
C_NOPE = 64
C_ROPE = 32
C_V = 64
C_Q_RANK = 256
C_KV_RANK = 128
COLS_A = 2 * A_WIDTH
COLS_B = (B_HEADS + 2 * B_KV_HEADS) * B_HEAD_DIM
COLS_C = C_Q_RANK + C_KV_RANK + C_ROPE
N_BRANCH = 3
PEER_HEADS = 8
PEER_N_KEYS = 128
PEER_HALF = 128
PEER_TOPK = 16

OFF_A = 0
OFF_B = OFF_A + COLS_A
OFF_C = OFF_B + COLS_B
OFF_G = OFF_C + C_Q_RANK + C_KV_RANK + LANE
W_IN_COLS = OFF_G + N_BRANCH * D_MODEL

CAND_PAIRS = [(a, b) for a in range(PEER_TOPK) for b in range(PEER_TOPK) if (a + 1) * (b + 1) <= PEER_TOPK]


def _cparams(sem):
    return pltpu.CompilerParams(dimension_semantics=sem, vmem_limit_bytes=VMEM_LIMIT)


def _const_spec(shape):
    nd = len(shape)
    return pl.BlockSpec(shape, lambda *_: (0,) * nd, pipeline_mode=pl.Buffered(1))


def _gelu(x):
    return 0.5 * x * (1.0 + jnp.tanh(0.7978845608028654 * (x + 0.044715 * (x * x * x))))


def _rms_rows(x, gain):
    return x * lax.rsqrt(jnp.mean(x * x, axis=-1, keepdims=True) + EPS) * gain


def _group_sumsq(x, ones_blockdiag):
    sq = x * x
    hi = sq.astype(BF16)
    lo = (sq - hi.astype(F32)).astype(BF16)
    return (jnp.dot(hi, ones_blockdiag, preferred_element_type=F32)
            + jnp.dot(lo, ones_blockdiag, preferred_element_type=F32))


def _swap_halves(x, group, lo_mask):
    w = x.shape[-1]
    half = group // 2
    return jnp.where(lo_mask, pltpu.roll(x, w - half, 1), pltpu.roll(x, half, 1))


def _ada_kernel(c_ref, w_ref, b_ref, o_ref):
    c = c_ref[...]
    ca = c * jax.nn.sigmoid(c)
    o_ref[0] = jnp.dot(ca, w_ref[0], preferred_element_type=F32,
                       precision=lax.Precision.HIGHEST) + b_ref[0]


def _ada_call(c_pad, w_ada, b_ada):
    depth, d, n = w_ada.shape
    tn = 1536
    return pl.pallas_call(
        _ada_kernel,
        grid=(depth, n // tn),
        in_specs=[
            pl.BlockSpec((8, d), lambda l, j: (0, 0)),
            pl.BlockSpec((1, d, tn), lambda l, j: (l, 0, j)),
            pl.BlockSpec((1, 1, tn), lambda l, j: (l, 0, j)),
        ],
        out_specs=pl.BlockSpec((1, 8, tn), lambda l, j: (l, 0, j)),
        out_shape=jax.ShapeDtypeStruct((depth, 8, n), F32),
        compiler_params=_cparams(("arbitrary", "arbitrary")),
        name="ada_mod",
    )(c_pad, w_ada, b_ada.reshape(depth, 1, n))


def _inproj_kernel(x_ref, mod_ref, n1g_ref, win_ref, avg_ref, ws_ref, bmat_ref, ones_ref,
                   qg_ref, kg_ref, cosb_ref, sinb_ref, cqg_ref, wuq_ref, ckvg_ref, wukv_ref,
                   cosc_ref, sinc_ref,
                   ya_ref, gates_ref, qb_ref, kb_ref, vtb_ref, qc_ref, kc_ref, vtc_ref):
    tm = x_ref.shape[0]
    d = D_MODEL
    shift = mod_ref[0, 0:1, :]
    scale = mod_ref[0, 1:2, :]
    h = (_rms_rows(x_ref[...], n1g_ref[...]) * (1.0 + scale) + shift).astype(BF16)

    za = jnp.dot(h, win_ref[:, OFF_A:OFF_A + COLS_A], preferred_element_type=F32)
    z = _gelu(za)
    u = z[:, :A_WIDTH]
    vn = _rms_rows(z[:, A_WIDTH:], avg_ref[...]).astype(BF16)
    grp = lax.broadcasted_iota(jnp.int32, (BLOCK, A_WIDTH), 1) // A_DIM
    for c in range(tm // BLOCK):
        vc = vn[c * BLOCK:(c + 1) * BLOCK]
        mixed = jnp.zeros((BLOCK, A_WIDTH), F32)
        for g in range(A_GROUPS):
            r = jnp.dot(ws_ref[g], vc, preferred_element_type=F32)
            mixed = jnp.where(grp == g, r, mixed)
        ya_ref[c * BLOCK:(c + 1) * BLOCK, :] = (
            u[c * BLOCK:(c + 1) * BLOCK] * (mixed + bmat_ref[...])).astype(BF16)

    zb = jnp.dot(h, win_ref[:, OFF_B:OFF_B + COLS_B], preferred_element_type=F32)
    qw = B_HEADS * B_HEAD_DIM
    kw = B_KV_HEADS * B_HEAD_DIM
    cosb = cosb_ref[...]
    sinb = sinb_ref[...]
    lane_q = lax.broadcasted_iota(jnp.int32, (tm, qw), 1)
    q = zb[:, :qw]
    qn = q * lax.rsqrt(_group_sumsq(q, ones_ref[...]) * (1.0 / B_HEAD_DIM) + EPS) * qg_ref[...]
    reps = qw // LANE
    qr = (qn * jnp.concatenate([cosb] * reps, axis=1)
          + _swap_halves(qn, B_HEAD_DIM, (lane_q % B_HEAD_DIM) < B_HEAD_DIM // 2)
          * jnp.concatenate([sinb] * reps, axis=1)) * (B_HEAD_DIM ** -0.5 * LOG2E)
    lane_t = lax.broadcasted_iota(jnp.int32, (tm, LANE), 1)
    for t in range(reps):
        tile = qr[:, t * LANE:(t + 1) * LANE]
        qb_ref[:, t * LANE:(t + 1) * LANE] = jnp.where(lane_t < B_HEAD_DIM, tile, 0.0).astype(BF16)
        qb_ref[:, (reps + t) * LANE:(reps + t + 1) * LANE] = (
            jnp.where(lane_t >= B_HEAD_DIM, tile, 0.0).astype(BF16))
    k = zb[:, qw:qw + kw]
    kn = k * lax.rsqrt(_group_sumsq(k, ones_ref[0:kw, 0:kw]) * (1.0 / B_HEAD_DIM) + EPS) * kg_ref[...]
    kr = kn * cosb + _swap_halves(kn, B_HEAD_DIM, (lane_t % B_HEAD_DIM) < B_HEAD_DIM // 2) * sinb
    kb_ref[...] = kr.astype(BF16)
    vtb_ref[0] = zb[:, qw + kw:].T.astype(BF16)

    zc = jnp.dot(h, win_ref[:, OFF_C:OFF_G], preferred_element_type=F32)
    cosc = cosc_ref[...]
    sinc = sinc_ref[...]
    rope_lo = C_NOPE + C_ROPE // 2
    cqn = _rms_rows(zc[:, :C_Q_RANK], cqg_ref[...]).astype(BF16)
    qf = jnp.dot(cqn, wuq_ref[...], preferred_element_type=F32)
    lane_c = lax.broadcasted_iota(jnp.int32, qf.shape, 1)
    qrot = (qf * jnp.concatenate([cosc] * C_HEADS, axis=1)
            + _swap_halves(qf, C_ROPE, (lane_c % LANE) < rope_lo)
            * jnp.concatenate([sinc] * C_HEADS, axis=1)) * ((C_NOPE + C_ROPE) ** -0.5 * LOG2E)
    qc_ref[...] = qrot.astype(BF16)
    ckvn = _rms_rows(zc[:, C_Q_RANK:C_Q_RANK + C_KV_RANK], ckvg_ref[...]).astype(BF16)
    kv = jnp.dot(ckvn, wukv_ref[...], preferred_element_type=F32)
    krope = zc[:, C_Q_RANK + C_KV_RANK:]
    krot = krope * cosc + _swap_halves(krope, C_ROPE, lane_t < rope_lo) * sinc
    kc_ref[...] = (kv[:, :C_HEADS * LANE] + jnp.concatenate([krot] * C_HEADS, axis=1)).astype(BF16)
    vtc_ref[0] = kv[:, C_HEADS * LANE:].T.astype(BF16)

    for n in range(N_BRANCH):
        zg = jnp.dot(h, win_ref[:, OFF_G + n * d:OFF_G + (n + 1) * d], preferred_element_type=F32)
        gates_ref[:, n * d:(n + 1) * d] = jax.nn.sigmoid(zg).astype(BF16)


def _inproj_call(x, mod, lw, tabs, tm, seq):
    t, d = x.shape
    nb = t // tm
    per_seq = seq // tm
    row = lambda i: (i, 0)
    pos = lambda i: (i % per_seq, 0)
    in_specs = [
        pl.BlockSpec((tm, d), row),
        pl.BlockSpec((1, 6, d), lambda i: (i // per_seq, 0, 0)),
        _const_spec((1, d)),
        _const_spec((d, W_IN_COLS)),
        _const_spec((1, A_WIDTH)),
        _const_spec((A_GROUPS, BLOCK, BLOCK)),
        _const_spec((BLOCK, A_WIDTH)),
        _const_spec((B_HEADS * B_HEAD_DIM, B_HEADS * B_HEAD_DIM)),
        _const_spec((1, B_HEADS * B_HEAD_DIM)),
        _const_spec((1, B_KV_HEADS * B_HEAD_DIM)),
        pl.BlockSpec((tm, LANE), pos),
        pl.BlockSpec((tm, LANE), pos),
        _const_spec((1, C_Q_RANK)),
        _const_spec((C_Q_RANK, C_HEADS * LANE)),
        _const_spec((1, C_KV_RANK)),
        _const_spec((C_KV_RANK, C_HEADS * (LANE + C_V))),
        pl.BlockSpec((tm, LANE), pos),
        pl.BlockSpec((tm, LANE), pos),
    ]
    out_shapes = (
        jax.ShapeDtypeStruct((t, A_WIDTH), BF16),
        jax.ShapeDtypeStruct((t, N_BRANCH * d), BF16),
        jax.ShapeDtypeStruct((t, B_HEADS * LANE), BF16),
        jax.ShapeDtypeStruct((t, LANE), BF16),
        jax.ShapeDtypeStruct((nb, B_KV_HEADS * B_HEAD_DIM, tm), BF16),
        jax.ShapeDtypeStruct((t, C_HEADS * LANE), BF16),
        jax.ShapeDtypeStruct((t, C_HEADS * LANE), BF16),
        jax.ShapeDtypeStruct((nb, C_HEADS * C_V, tm), BF16),
    )
    out_specs = (
        pl.BlockSpec((tm, A_WIDTH), row),
        pl.BlockSpec((tm, N_BRANCH * d), row),
        pl.BlockSpec((tm, B_HEADS * LANE), row),
        pl.BlockSpec((tm, LANE), row),
        pl.BlockSpec((1, B_KV_HEADS * B_HEAD_DIM, tm), lambda i: (i, 0, 0)),
        pl.BlockSpec((tm, C_HEADS * LANE), row),
        pl.BlockSpec((tm, C_HEADS * LANE), row),
        pl.BlockSpec((1, C_HEADS * C_V, tm), lambda i: (i, 0, 0)),
    )
    return pl.pallas_call(
        _inproj_kernel,
        grid=(nb,),
        in_specs=in_specs,
        out_specs=out_specs,
        out_shape=out_shapes,
        compiler_params=_cparams(("arbitrary",)),
        name="in_proj",
    )(x, mod, lw["n1g"], lw["w_in"], lw["a_v_gain"], lw["w_s"], lw["bmat"], tabs["ones"],
      lw["qg"], lw["kg"], tabs["cosb"], tabs["sinb"], lw["cqg"], lw["w_uq"], lw["ckvg"], lw["w_ukv"],
      tabs["cosc"], tabs["sinc"])


def _attn_kernel(q_ref, k_ref, vt_ref, o_ref, s_ref, *, heads_share_kv, dv):
    tq = q_ref.shape[0]
    n_chunks, _, tk = vt_ref.shape
    ones = jnp.ones((16, tk), BF16)
    hs = range(2)
    qs = [q_ref[:, hh * LANE:(hh + 1) * LANE] for hh in hs]

    def scores(c, slot):
        start = pl.multiple_of(c * tk, tk)
        s_t = [lax.dot_general(k_ref[pl.ds(start, tk), (0 if heads_share_kv else hh * LANE):
                                     (LANE if heads_share_kv else (hh + 1) * LANE)],
                               qs[hh], (((1,), (1,)), ((), ())), preferred_element_type=F32) for hh in hs]
        for hh in hs:
            s_ref[slot, hh] = s_t[hh]
        return tuple(jnp.max(s_t[hh], axis=0, keepdims=True) for hh in hs)

    def accumulate(c, slot, state, cmax):
        m_new = [jnp.maximum(state[hh][0], cmax[hh]) for hh in hs]
        p = [jnp.exp2(s_ref[slot, hh] - m_new[hh]).astype(BF16) for hh in hs]
        alpha = [jnp.exp2(state[hh][0] - m_new[hh]) for hh in hs]
        vt = [jnp.concatenate([vt_ref[c, (0 if heads_share_kv else hh * dv):
                                      (dv if heads_share_kv else (hh + 1) * dv), :], ones], axis=0) for hh in hs]
        pv = [jnp.dot(vt[hh], p[hh], preferred_element_type=F32) for hh in hs]
        return tuple((m_new[hh], state[hh][1] * alpha[hh] + pv[hh]) for hh in hs)

    def body(i, carry):
        state, cmax = carry
        c = 2 * i
        cmax1 = scores(c + 1, 1)
        state = accumulate(c, 0, state, cmax)
        cmax0 = scores(jnp.minimum(c + 2, n_chunks - 1), 0)
        state = accumulate(c + 1, 1, state, cmax1)
        return state, cmax0

    state = tuple((jnp.full((1, tq), -jnp.inf, F32), jnp.zeros((dv + 16, tq), F32)) for _ in hs)
    cmax = scores(0, 0)
    if n_chunks > 1:
        assert n_chunks % 2 == 0
        state, _ = lax.fori_loop(0, n_chunks // 2, body, (state, cmax))
    else:
        state = accumulate(0, 0, state, cmax)
    outs = [acc[:dv] * (1.0 / acc[dv:dv + 1]) for _, acc in state]
    o_ref[...] = jnp.concatenate(outs, axis=0).T.astype(BF16)


def _attn_call(q, k, vt, batch, seq, tq, heads_share_kv, q_heads_per_kv):
    t = q.shape[0]
    n_heads = q.shape[1] // LANE
    tk = vt.shape[2]
    dv = 64
    n_chunks = seq // tk
    nq = seq // tq
    if heads_share_kv:
        k_spec = pl.BlockSpec((seq, LANE), lambda b, j, i: (b, 0))
        pairs_per_kv = q_heads_per_kv // 2
        vt_spec = pl.BlockSpec((n_chunks, dv, tk), lambda b, j, i: (b, j // pairs_per_kv, 0))
    else:
        k_spec = pl.BlockSpec((seq, 2 * LANE), lambda b, j, i: (b, j))
        vt_spec = pl.BlockSpec((n_chunks, 2 * dv, tk), lambda b, j, i: (b, j, 0))
    return pl.pallas_call(
        functools.partial(_attn_kernel, heads_share_kv=heads_share_kv, dv=dv),
        grid=(batch, n_heads // 2, nq),
        in_specs=[pl.BlockSpec((tq, 2 * LANE), lambda b, j, i: (b * nq + i, j)), k_spec, vt_spec],
        out_specs=pl.BlockSpec((tq, 2 * dv), lambda b, j, i: (b * nq + i, j)),
        out_shape=jax.ShapeDtypeStruct((t, n_heads * dv), BF16),
        compiler_params=_cparams(("arbitrary", "arbitrary", "arbitrary")),
        scratch_shapes=[pltpu.VMEM((2, 2, tk, tq), F32)],
        name="attn_shared_kv" if heads_share_kv else "attn_latent",
    )(q, k, vt)


def _merge_kernel(ya_ref, yb_ref, yc_ref, gates_ref, x_ref, mod_ref, n2g_ref, wpa_ref, wpb_ref,
                  wpc_ref, wout_ref, wq_ref, sk_ref,
                  x1_ref, h2_ref, lrow_ref, e1_ref, r2_ref, e2_ref, vals_ref, s_scr, r1_scr):
    d = D_MODEL
    merged = (gates_ref[:, 0:d].astype(F32) * jnp.dot(ya_ref[...], wpa_ref[...], preferred_element_type=F32)
              + gates_ref[:, d:2 * d].astype(F32) * jnp.dot(yb_ref[...], wpb_ref[...], preferred_element_type=F32)
              + gates_ref[:, 2 * d:3 * d].astype(F32) * jnp.dot(yc_ref[...], wpc_ref[...], preferred_element_type=F32))
    x1 = x_ref[...] + mod_ref[0, 2:3, :] * jnp.dot(merged.astype(BF16), wout_ref[...],
                                                   preferred_element_type=F32)
    x1_ref[...] = x1
    h2 = (_rms_rows(x1, n2g_ref[...]) * (1.0 + mod_ref[0, 4:5, :]) + mod_ref[0, 3:4, :]).astype(BF16)
    h2_ref[...] = h2
    qk = jnp.dot(h2, wq_ref[...], preferred_element_type=F32).astype(BF16)

    for h in range(PEER_HEADS):
        for p in range(2):
            idx = 2 * h + p
            s = lax.dot_general(sk_ref[idx], qk[:, idx * PEER_HALF:(idx + 1) * PEER_HALF],
                                (((1,), (1,)), ((), ())), preferred_element_type=F32)
            s_scr[p, h] = s
            for t0 in range(0, s.shape[1], LANE):
                st = s[:, t0:t0 + LANE]
                rank = jnp.full(st.shape, float(PEER_TOPK), F32)
                for a in range(PEER_TOPK):
                    mx = jnp.max(st, axis=0, keepdims=True)
                    vals_ref[p, a, h:h + 1, t0:t0 + LANE] = mx
                    hit = st == mx
                    rank = jnp.where(hit, float(a), rank)
                    st = jnp.where(hit, -jnp.inf, st)
                if p == 0:
                    r1_scr[h, :, t0:t0 + LANE] = rank
                else:
                    r2_ref[h, :, t0:t0 + LANE] = rank.astype(BF16)

    v1 = [vals_ref[0, a] for a in range(PEER_TOPK)]
    v2 = [vals_ref[1, a] for a in range(PEER_TOPK)]
    cands = [v1[a] + v2[b] for a, b in CAND_PAIRS]
    work = list(cands)
    tau = None
    for it in range(PEER_TOPK):
        mx = functools.reduce(jnp.maximum, work)
        if it == PEER_TOPK - 1:
            tau = mx
        else:
            work = [jnp.where(w == mx, -jnp.inf, w) for w in work]
    top = v1[0] + v2[0]
    zsum = functools.reduce(jnp.add, [jnp.where(c >= tau, jnp.exp(c - top), 0.0) for c in cands])
    rz = 1.0 / zsum
    count = [functools.reduce(jnp.add, [jnp.where(c >= tau, 1.0, 0.0)
                                         for c, (a2, _) in zip(cands, CAND_PAIRS) if a2 == a])
             for a in range(PEER_TOPK)]
    for h in range(PEER_HEADS):
        r1 = r1_scr[h]
        lrow = jnp.zeros(r1.shape, F32)
        for a in range(PEER_TOPK):
            lrow = jnp.where(r1 == float(a), count[a][h:h + 1, :], lrow)
        lrow_ref[h] = lrow
        e1_ref[h] = jnp.exp(s_scr[0, h] - v1[0][h:h + 1, :])
        e2_ref[h] = (jnp.exp(s_scr[1, h] - v2[0][h:h + 1, :]) * rz[h:h + 1, :]).astype(BF16)


def _merge_call(ya, yb, yc, gates, x, mod, lw, tm, seq):
    t, d = x.shape
    nb = t // tm
    per_seq = seq // tm
    row = lambda i: (i, 0)
    hk = pl.BlockSpec((PEER_HEADS, PEER_N_KEYS, tm), lambda i: (0, 0, i))
    hk_f32 = jax.ShapeDtypeStruct((PEER_HEADS, PEER_N_KEYS, t), F32)
    hk_bf16 = jax.ShapeDtypeStruct((PEER_HEADS, PEER_N_KEYS, t), BF16)
    return pl.pallas_call(
        _merge_kernel,
        grid=(nb,),
        in_specs=[
            pl.BlockSpec((tm, A_WIDTH), row),
            pl.BlockSpec((tm, A_WIDTH), row),
            pl.BlockSpec((tm, A_WIDTH), row),
            pl.BlockSpec((tm, N_BRANCH * d), row),
            pl.BlockSpec((tm, d), row),
            pl.BlockSpec((1, 6, d), lambda i: (i // per_seq, 0, 0)),
            _const_spec((1, d)),
            _const_spec((A_WIDTH, d)),
            _const_spec((A_WIDTH, d)),
            _const_spec((A_WIDTH, d)),
            _const_spec((d, d)),
            _const_spec((d, 2 * PEER_HEADS * PEER_HALF)),
            _const_spec((2 * PEER_HEADS, PEER_N_KEYS, PEER_HALF)),
        ],
        out_specs=(
            pl.BlockSpec((tm, d), row),
            pl.BlockSpec((tm, d), row),
            hk, hk, hk, hk,
        ),
        out_shape=(
            jax.ShapeDtypeStruct((t, d), F32),
            jax.ShapeDtypeStruct((t, d), BF16),
            hk_f32, hk_f32, hk_bf16, hk_bf16,
        ),
        scratch_shapes=[pltpu.VMEM((2, PEER_TOPK, PEER_HEADS, tm), F32),
                        pltpu.VMEM((2, PEER_HEADS, PEER_N_KEYS, tm), F32),
                        pltpu.VMEM((PEER_HEADS, PEER_N_KEYS, tm), F32)],
        compiler_params=_cparams(("arbitrary",)),
        name="merge_route",
    )(ya, yb, yc, gates, x, mod, lw["n2g"], lw["w_pa"], lw["w_pb"], lw["w_pc"], lw["w_out"],
      lw["w_query"], lw["sub_keys"])


def _expert_kernel(h2_ref, lrow_ref, e1_ref, r2_ref, e2_ref, u_ref, vt_ref, x1_ref, mod_ref,
                   fg_ref, o_ref, acc_ref, w_ref, *, final_norm):
    ch = pl.program_id(1)
    rows = lrow_ref.shape[1]
    tm = h2_ref.shape[0]
    pack_rows = 16

    @pl.when(ch == 0)
    def _():
        acc_ref[...] = jnp.zeros_like(acc_ref)

    def row_tile(ref, h, r):
        row = jnp.broadcast_to(ref[h, r:r + 1, :], (pack_rows, tm)).astype(BF16)
        return jnp.concatenate([row] * (PEER_N_KEYS // pack_rows), axis=0)

    act = lax.dot_general(u_ref[...], h2_ref[...], (((1,), (1,)), ((), ())), preferred_element_type=F32)
    zero = jnp.zeros((PEER_N_KEYS, tm), BF16)
    for r in range(rows):
        gate = None
        for h in range(PEER_HEADS):
            g = jnp.where(r2_ref[h] < row_tile(lrow_ref, h, r), e2_ref[h], zero) * row_tile(e1_ref, h, r)
            gate = g if gate is None else gate + g
        w_ref[r * PEER_N_KEYS:(r + 1) * PEER_N_KEYS, :] = (
            gate * _gelu(act[r * PEER_N_KEYS:(r + 1) * PEER_N_KEYS, :].astype(BF16)))
    acc_ref[...] += jnp.dot(vt_ref[...], w_ref[...], preferred_element_type=F32)

    @pl.when(ch == pl.num_programs(1) - 1)
    def _():
        x2 = x1_ref[...] + mod_ref[0, 5:6, :] * acc_ref[...].T
        if final_norm:
            x2 = _rms_rows(x2, fg_ref[...])
        o_ref[...] = x2


def _expert_call(h2, lrow, e1, r2, e2, u, vt, x1, mod, final_gain, tm, ec, seq, final_norm):
    t, d = x1.shape
    n_exp = u.shape[0]
    rows = ec // PEER_N_KEYS
    per_seq = seq // tm
    tok = lambda i, c: (i, 0)
    chunk_rows = pl.BlockSpec((PEER_HEADS, rows, tm), lambda i, c: (0, c, i))
    all_rows = pl.BlockSpec((PEER_HEADS, PEER_N_KEYS, tm), lambda i, c: (0, 0, i))
    return pl.pallas_call(
        functools.partial(_expert_kernel, final_norm=final_norm),
        grid=(t // tm, n_exp // ec),
        in_specs=[
            pl.BlockSpec((tm, d), tok),
            chunk_rows, chunk_rows, all_rows, all_rows,
            pl.BlockSpec((ec, d), lambda i, c: (c, 0)),
            pl.BlockSpec((d, ec), lambda i, c: (0, c)),
            pl.BlockSpec((tm, d), tok),
            pl.BlockSpec((1, 6, d), lambda i, c: (i // per_seq, 0, 0)),
            pl.BlockSpec((1, d), lambda i, c: (0, 0)),
        ],
        out_specs=pl.BlockSpec((tm, d), tok),
        out_shape=jax.ShapeDtypeStruct((t, d), F32),
        scratch_shapes=[pltpu.VMEM((d, tm), F32), pltpu.VMEM((ec, tm), BF16)],
        compiler_params=_cparams(("arbitrary", "arbitrary")),
        name="peer_dense",
    )(h2, lrow, e1, r2, e2, u, vt, x1, mod, final_gain)


def _w_in_column_map():
    src = np.full((W_IN_COLS,), -1, np.int64)
    src[OFF_A:OFF_A + COLS_A] = np.arange(COLS_A)
    half = B_HEAD_DIM // 2
    pair = np.concatenate([np.arange(0, B_HEAD_DIM, 2), np.arange(1, B_HEAD_DIM, 2)])
    tiles = B_HEADS * B_HEAD_DIM // LANE
    for t in range(tiles):
        for side, head in enumerate((t, tiles + t)):
            dst = OFF_B + t * LANE + side * B_HEAD_DIM
            src[dst:dst + B_HEAD_DIM] = COLS_A + head * B_HEAD_DIM + pair
    qw = B_HEADS * B_HEAD_DIM
    for kv in range(B_KV_HEADS):
        dst = OFF_B + qw + kv * B_HEAD_DIM
        src[dst:dst + B_HEAD_DIM] = COLS_A + qw + kv * B_HEAD_DIM + pair
    kw = B_KV_HEADS * B_HEAD_DIM
    src[OFF_B + qw + kw:OFF_B + qw + 2 * kw] = COLS_A + qw + kw + np.arange(kw)
    base_c = COLS_A + COLS_B
    src[OFF_C:OFF_C + C_Q_RANK + C_KV_RANK] = base_c + np.arange(C_Q_RANK + C_KV_RANK)
    rope_src = base_c + C_Q_RANK + C_KV_RANK
    dst = OFF_C + C_Q_RANK + C_KV_RANK + C_NOPE
    src[dst:dst + C_ROPE // 2] = rope_src + np.arange(0, C_ROPE, 2)
    src[dst + C_ROPE // 2:dst + C_ROPE] = rope_src + np.arange(1, C_ROPE, 2)
    src[OFF_G:] = base_c + COLS_C + np.arange(N_BRANCH * D_MODEL)
    del half
    return src


def _take_cols(w, src):
    mask = jnp.asarray(src >= 0, w.dtype)
    return jnp.take(w, jnp.asarray(np.maximum(src, 0)), axis=-1) * mask


def _uq_column_map():
    src = np.full((C_HEADS * LANE,), -1, np.int64)
    per = C_NOPE + C_ROPE
    for h in range(C_HEADS):
        src[h * LANE:h * LANE + C_NOPE] = h * per + np.arange(C_NOPE)
        src[h * LANE + C_NOPE:h * LANE + C_NOPE + C_ROPE // 2] = h * per + C_NOPE + np.arange(0, C_ROPE, 2)
        src[h * LANE + C_NOPE + C_ROPE // 2:h * LANE + per] = h * per + C_NOPE + np.arange(1, C_ROPE, 2)
    return src


def _ukv_column_map():
    src = np.full((C_HEADS * (LANE + C_V),), -1, np.int64)
    per = C_NOPE + C_V
    for h in range(C_HEADS):
        src[h * LANE:h * LANE + C_NOPE] = h * per + np.arange(C_NOPE)
        src[C_HEADS * LANE + h * C_V:C_HEADS * LANE + (h + 1) * C_V] = h * per + C_NOPE + np.arange(C_V)
    return src


def _rope_tables(seq):
    rows = seq // GRID_W
    row = jnp.repeat(jnp.arange(rows, dtype=F32), GRID_W)
    col = jnp.tile(jnp.arange(GRID_W, dtype=F32), rows)

    def cos_sin(d_rot):
        half = d_rot // 2
        freq = ROPE_THETA ** (-jnp.arange(0, half, 2, dtype=F32) / half)
        ang = jnp.concatenate([row[:, None] * freq, col[:, None] * freq], axis=-1)
        return jnp.cos(ang), jnp.sin(ang)

    cb, sb = cos_sin(B_HEAD_DIM)
    cosb = jnp.tile(jnp.concatenate([cb, cb], axis=1), (1, LANE // B_HEAD_DIM))
    sinb = jnp.tile(jnp.concatenate([-sb, sb], axis=1), (1, LANE // B_HEAD_DIM))
    cc, sc = cos_sin(C_ROPE)
    one = jnp.ones((seq, C_NOPE), F32)
    pad = LANE - C_NOPE - C_ROPE
    cosc = jnp.concatenate([one, cc, cc, jnp.ones((seq, pad), F32)], axis=1)
    sinc = jnp.concatenate([0.0 * one, -sc, sc, jnp.zeros((seq, pad), F32)], axis=1)
    return cosb, sinb, cosc, sinc


def kernel(x, c, w_ada, b_ada, norm1_gain, w_in, a_v_gain, a_w_s, a_b_s, b_q_gain, b_k_gain,
           c_q_gain, c_w_uq, c_kv_gain, c_w_ukv, w_pa, w_pb, w_pc, w_out, norm2_gain,
           peer_w_query, peer_sub_keys, peer_u, peer_v, final_gain):
    batch, seq, d = x.shape
    depth = w_ada.shape[0]
    t = batch * seq
    assert d == D_MODEL and seq % GRID_W == 0
    tm = min(512, seq)
    tq = min(512, seq)
    tm_merge = min(256, seq)
    ec = 1024
    assert seq % tm == 0 and seq % tq == 0

    cosb, sinb, cosc, sinc = _rope_tables(seq)
    qw = B_HEADS * B_HEAD_DIM
    ones = jnp.asarray(np.kron(np.eye(qw // B_HEAD_DIM), np.ones((B_HEAD_DIM, B_HEAD_DIM))), BF16)
    tabs = dict(cosb=cosb, sinb=sinb, cosc=cosc, sinc=sinc, ones=ones)

    c_pad = jnp.zeros((8, d), F32).at[:batch].set(c)
    mod_all = _ada_call(c_pad, w_ada, b_ada)[:, :batch, :].reshape(depth, batch, 6, d)

    in_map, uq_map, ukv_map = _w_in_column_map(), _uq_column_map(), _ukv_column_map()
    pair = np.concatenate([np.arange(0, B_HEAD_DIM, 2), np.arange(1, B_HEAD_DIM, 2)])
    w_in_r = _take_cols(w_in, in_map).astype(BF16)
    w_uq_r = _take_cols(c_w_uq, uq_map).astype(BF16)
    w_ukv_r = _take_cols(c_w_ukv, ukv_map).astype(BF16)
    u_b = peer_u.astype(BF16)
    vt_b = jnp.swapaxes(peer_v, 1, 2).astype(BF16)

    xf = x.reshape(t, d)
    for l in range(depth):
        lw = dict(
            n1g=norm1_gain[l][None, :],
            w_in=w_in_r[l],
            a_v_gain=a_v_gain[l][None, :],
            w_s=a_w_s[l].astype(BF16),
            bmat=jnp.repeat(a_b_s[l].T, A_DIM, axis=1),
            qg=jnp.tile(b_q_gain[l][pair], B_HEADS)[None, :],
            kg=jnp.tile(b_k_gain[l][pair], B_KV_HEADS)[None, :],
            cqg=c_q_gain[l][None, :],
            w_uq=w_uq_r[l],
            ckvg=c_kv_gain[l][None, :],
            w_ukv=w_ukv_r[l],
            n2g=norm2_gain[l][None, :],
            w_pa=w_pa[l].astype(BF16),
            w_pb=w_pb[l].astype(BF16),
            w_pc=w_pc[l].astype(BF16),
            w_out=w_out[l].astype(BF16),
            w_query=peer_w_query[l].astype(BF16),
            sub_keys=peer_sub_keys[l].reshape(2 * PEER_HEADS, PEER_N_KEYS, PEER_HALF).astype(BF16),
        )
        mod = mod_all[l]
        ya, gates, qb, kb, vtb, qc, kc, vtc = _inproj_call(xf, mod, lw, tabs, tm, seq)
        yb = _attn_call(qb, kb, vtb, batch, seq, tq, True, B_HEADS // B_KV_HEADS)
        yc = _attn_call(qc, kc, vtc, batch, seq, tq, False, 1)
        x1, h2, lrow, e1, r2, e2 = _merge_call(ya, yb, yc, gates, xf, mod, lw, tm_merge, seq)
        xf = _expert_call(h2, lrow, e1, r2, e2, u_b[l], vt_b[l], x1, mod, final_gain[None, :],
                          tm, ec, seq, l == depth - 1)
    return xf.reshape(batch, seq, d)
```

```python
import functools

import numpy as np
import jax
import jax.numpy as jnp
from jax import lax
from jax.experimental import pallas as pl
from jax.experimental.pallas import tpu as pltpu

F32 = jnp.float32
BF16 = jnp.bfloat16

LANE = 128
VMEM_LIMIT = 56 * 1024 * 1024

D_MODEL = 1024
GRID_W = 64
BLOCK = 128
EPS = 1e-6
ROPE_THETA = 10000.0
LOG2E = 1.4426950408889634

A_GROUPS = 8
A_DIM = 64
A_WIDTH = A_GROUPS * A_DIM
B_HEADS = 8
B_KV_HEADS = 2
B_HEAD_DIM = 64
C_HEADS = 8
C_NOPE = 64
C_ROPE = 32
C_V = 64
C_Q_RANK = 256
C_KV_RANK = 128
COLS_A = 2 * A_WIDTH
COLS_B = (B_HEADS + 2 * B_KV_HEADS) * B_HEAD_DIM
COLS_C = C_Q_RANK + C_KV_RANK + C_ROPE
N_BRANCH = 3
PEER_HEADS = 8
PEER_N_KEYS = 128
PEER_HALF = 128
PEER_TOPK = 16

OFF_A = 0
OFF_B = OFF_A + COLS_A
OFF_C = OFF_B + COLS_B
OFF_G = OFF_C + C_Q_RANK + C_KV_RANK + LANE
W_IN_COLS = OFF_G + N_BRANCH * D_MODEL

CAND_PAIRS = [(a, b) for a in range(PEER_TOPK) for b in range(PEER_TOPK) if (a + 1) * (b + 1) <= PEER_TOPK]


def _cparams(sem):
    return pltpu.CompilerParams(dimension_semantics=sem, vmem_limit_bytes=VMEM_LIMIT)


def _const_spec(shape):
    nd = len(shape)
    return pl.BlockSpec(shape, lambda *_: (0,) * nd, pipeline_mode=pl.Buffered(1))


def _gelu(x):
    return 0.5 * x * (1.0 + jnp.tanh(0.7978845608028654 * (x + 0.044715 * (x * x * x))))


def _rms_rows(x, gain):
    return x * lax.rsqrt(jnp.mean(x * x, axis=-1, keepdims=True) + EPS) * gain


def _group_sumsq(x, ones_blockdiag):
    sq = x * x
    hi = sq.astype(BF16)
    lo = (sq - hi.astype(F32)).astype(BF16)
    return (jnp.dot(hi, ones_blockdiag, preferred_element_type=F32)
            + jnp.dot(lo, ones_blockdiag, preferred_element_type=F32))


def _swap_halves(x, group, lo_mask):
    w = x.shape[-1]
    half = group // 2
    return jnp.where(lo_mask, pltpu.roll(x, w - half, 1), pltpu.roll(x, half, 1))


def _ada_kernel(c_ref, w_ref, b_ref, o_ref):
    c = c_ref[...]
    ca = c * jax.nn.sigmoid(c)
    o_ref[0] = jnp.dot(ca, w_ref[0], preferred_element_type=F32,
                       precision=lax.Precision.HIGHEST) + b_ref[0]


def _ada_call(c_pad, w_ada, b_ada):
    depth, d, n = w_ada.shape
    tn = 1536
    return pl.pallas_call(
        _ada_kernel,
        grid=(depth, n // tn),
        in_specs=[
            pl.BlockSpec((8, d), lambda l, j: (0, 0)),
            pl.BlockSpec((1, d, tn), lambda l, j: (l, 0, j)),
            pl.BlockSpec((1, 1, tn), lambda l, j: (l, 0, j)),
        ],
        out_specs=pl.BlockSpec((1, 8, tn), lambda l, j: (l, 0, j)),
        out_shape=jax.ShapeDtypeStruct((depth, 8, n), F32),
        compiler_params=_cparams(("arbitrary", "arbitrary")),
        name="ada_mod",
    )(c_pad, w_ada, b_ada.reshape(depth, 1, n))


def _inproj_kernel(x_ref, mod_ref, n1g_ref, win_ref, avg_ref, ws_ref, bmat_ref, ones_ref,
                   qg_ref, kg_ref, cosb_ref, sinb_ref, cqg_ref, wuq_ref, ckvg_ref, wukv_ref,
                   cosc_ref, sinc_ref,
                   ya_ref, gates_ref, qb_ref, kb_ref, vtb_ref, qc_ref, kc_ref, vtc_ref):
    tm = x_ref.shape[0]
    d = D_MODEL
    shift = mod_ref[0, 0:1, :]
    scale = mod_ref[0, 1:2, :]
    h = (_rms_rows(x_ref[...], n1g_ref[...]) * (1.0 + scale) + shift).astype(BF16)

    za = jnp.dot(h, win_ref[:, OFF_A:OFF_A + COLS_A], preferred_element_type=F32)
    z = _gelu(za)
    u = z[:, :A_WIDTH]
    vn = _rms_rows(z[:, A_WIDTH:], avg_ref[...]).astype(BF16)
    grp = lax.broadcasted_iota(jnp.int32, (BLOCK, A_WIDTH), 1) // A_DIM
    for c in range(tm // BLOCK):
        vc = vn[c * BLOCK:(c + 1) * BLOCK]
        mixed = jnp.zeros((BLOCK, A_WIDTH), F32)
        for g in range(A_GROUPS):
            r = jnp.dot(ws_ref[g], vc, preferred_element_type=F32)
            mixed = jnp.where(grp == g, r, mixed)
        ya_ref[c * BLOCK:(c + 1) * BLOCK, :] = (
            u[c * BLOCK:(c + 1) * BLOCK] * (mixed + bmat_ref[...])).astype(BF16)

    zb = jnp.dot(h, win_ref[:, OFF_B:OFF_B + COLS_B], preferred_element_type=F32)
    qw = B_HEADS * B_HEAD_DIM
    kw = B_KV_HEADS * B_HEAD_DIM
    cosb = cosb_ref[...]
    sinb = sinb_ref[...]
    lane_q = lax.broadcasted_iota(jnp.int32, (tm, qw), 1)
    q = zb[:, :qw]
    qn = q * lax.rsqrt(_group_sumsq(q, ones_ref[...]) * (1.0 / B_HEAD_DIM) + EPS) * qg_ref[...]
    reps = qw // LANE
    qr = (qn * jnp.concatenate([cosb] * reps, axis=1)
          + _swap_halves(qn, B_HEAD_DIM, (lane_q % B_HEAD_DIM) < B_HEAD_DIM // 2)
          * jnp.concatenate([sinb] * reps, axis=1)) * (B_HEAD_DIM ** -0.5 * LOG2E)
    lane_t = lax.broadcasted_iota(jnp.int32, (tm, LANE), 1)
    for t in range(reps):
        tile = qr[:, t * LANE:(t + 1) * LANE]
        qb_ref[:, t * LANE:(t + 1) * LANE] = jnp.where(lane_t < B_HEAD_DIM, tile, 0.0).astype(BF16)
        qb_ref[:, (reps + t) * LANE:(reps + t + 1) * LANE] = (
            jnp.where(lane_t >= B_HEAD_DIM, tile, 0.0).astype(BF16))
    k = zb[:, qw:qw + kw]
    kn = k * lax.rsqrt(_group_sumsq(k, ones_ref[0:kw, 0:kw]) * (1.0 / B_HEAD_DIM) + EPS) * kg_ref[...]
    kr = kn * cosb + _swap_halves(kn, B_HEAD_DIM, (lane_t % B_HEAD_DIM) < B_HEAD_DIM // 2) * sinb
    kb_ref[...] = kr.astype(BF16)
    vtb_ref[0] = zb[:, qw + kw:].T.astype(BF16)

    zc = jnp.dot(h, win_ref[:, OFF_C:OFF_G], preferred_element_type=F32)
    cosc = cosc_ref[...]
    sinc = sinc_ref[...]
    rope_lo = C_NOPE + C_ROPE // 2
    cqn = _rms_rows(zc[:, :C_Q_RANK], cqg_ref[...]).astype(BF16)
    qf = jnp.dot(cqn, wuq_ref[...], preferred_element_type=F32)
    lane_c = lax.broadcasted_iota(jnp.int32, qf.shape, 1)
    qrot = (qf * jnp.concatenate([cosc] * C_HEADS, axis=1)
            + _swap_halves(qf, C_ROPE, (lane_c % LANE) < rope_lo)
            * jnp.concatenate([sinc] * C_HEADS, axis=1)) * ((C_NOPE + C_ROPE) ** -0.5 * LOG2E)
    qc_ref[...] = qrot.astype(BF16)
    ckvn = _rms_rows(zc[:, C_Q_RANK:C_Q_RANK + C_KV_RANK], ckvg_ref[...]).astype(BF16)
    kv = jnp.dot(ckvn, wukv_ref[...], preferred_element_type=F32)
    krope = zc[:, C_Q_RANK + C_KV_RANK:]
    krot = krope * cosc + _swap_halves(krope, C_ROPE, lane_t < rope_lo) * sinc
    kc_ref[...] = (kv[:, :C_HEADS * LANE] + jnp.concatenate([krot] * C_HEADS, axis=1)).astype(BF16)
    vtc_ref[0] = kv[:, C_HEADS * LANE:].T.astype(BF16)

    for n in range(N_BRANCH):
        zg = jnp.dot(h, win_ref[:, OFF_G + n * d:OFF_G + (n + 1) * d], preferred_element_type=F32)
        gates_ref[:, n * d:(n + 1) * d] = jax.nn.sigmoid(zg).astype(BF16)


def _inproj_call(x, mod, lw, tabs, tm, seq):
    t, d = x.shape
    nb = t // tm
    per_seq = seq // tm
    row = lambda i: (i, 0)
    pos = lambda i: (i % per_seq, 0)
    in_specs = [
        pl.BlockSpec((tm, d), row),
        pl.BlockSpec((1, 6, d), lambda i: (i // per_seq, 0, 0)),
        _const_spec((1, d)),
        _const_spec((d, W_IN_COLS)),
        _const_spec((1, A_WIDTH)),
        _const_spec((A_GROUPS, BLOCK, BLOCK)),
        _const_spec((BLOCK, A_WIDTH)),
        _const_spec((B_HEADS * B_HEAD_DIM, B_HEADS * B_HEAD_DIM)),
        _const_spec((1, B_HEADS * B_HEAD_DIM)),
        _const_spec((1, B_KV_HEADS * B_HEAD_DIM)),
        pl.BlockSpec((tm, LANE), pos),
        pl.BlockSpec((tm, LANE), pos),
        _const_spec((1, C_Q_RANK)),
        _const_spec((C_Q_RANK, C_HEADS * LANE)),
        _const_spec((1, C_KV_RANK)),
        _const_spec((C_KV_RANK, C_HEADS * (LANE + C_V))),
        pl.BlockSpec((tm, LANE), pos),
        pl.BlockSpec((tm, LANE), pos),
    ]
    out_shapes = (
        jax.ShapeDtypeStruct((t, A_WIDTH), BF16),
        jax.ShapeDtypeStruct((t, N_BRANCH * d), BF16),
        jax.ShapeDtypeStruct((t, B_HEADS * LANE), BF16),
        jax.ShapeDtypeStruct((t, LANE), BF16),
        jax.ShapeDtypeStruct((nb, B_KV_HEADS * B_HEAD_DIM, tm), BF16),
        jax.ShapeDtypeStruct((t, C_HEADS * LANE), BF16),
        jax.ShapeDtypeStruct((t, C_HEADS * LANE), BF16),
        jax.ShapeDtypeStruct((nb, C_HEADS * C_V, tm), BF16),
    )
    out_specs = (
        pl.BlockSpec((tm, A_WIDTH), row),
        pl.BlockSpec((tm, N_BRANCH * d), row),
        pl.BlockSpec((tm, B_HEADS * LANE), row),
        pl.BlockSpec((tm, LANE), row),
        pl.BlockSpec((1, B_KV_HEADS * B_HEAD_DIM, tm), lambda i: (i, 0, 0)),
        pl.BlockSpec((tm, C_HEADS * LANE), row),
        pl.BlockSpec((tm, C_HEADS * LANE), row),
        pl.BlockSpec((1, C_HEADS * C_V, tm), lambda i: (i, 0, 0)),
    )
    return pl.pallas_call(
        _inproj_kernel,
        grid=(nb,),
        in_specs=in_specs,
        out_specs=out_specs,
        out_shape=out_shapes,
        compiler_params=_cparams(("arbitrary",)),
        name="in_proj",
    )(x, mod, lw["n1g"], lw["w_in"], lw["a_v_gain"], lw["w_s"], lw["bmat"], tabs["ones"],
      lw["qg"], lw["kg"], tabs["cosb"], tabs["sinb"], lw["cqg"], lw["w_uq"], lw["ckvg"], lw["w_ukv"],
      tabs["cosc"], tabs["sinc"])


def _attn_kernel(q_ref, k_ref, vt_ref, o_ref, s_ref, *, heads_share_kv, dv):
    tq = q_ref.shape[0]
    n_chunks, _, tk = vt_ref.shape
    ones = jnp.ones((16, tk), BF16)
    hs = range(2)
    qs = [q_ref[:, hh * LANE:(hh + 1) * LANE] for hh in hs]

    def scores(c, slot):
        start = pl.multiple_of(c * tk, tk)
        s_t = [lax.dot_general(k_ref[pl.ds(start, tk), (0 if heads_share_kv else hh * LANE):
                                     (LANE if heads_share_kv else (hh + 1) * LANE)],
                               qs[hh], (((1,), (1,)), ((), ())), preferred_element_type=F32) for hh in hs]
        for hh in hs:
            s_ref[slot, hh] = s_t[hh]
        return tuple(jnp.max(s_t[hh], axis=0, keepdims=True) for hh in hs)

    def accumulate(c, slot, state, cmax):
        m_new = [jnp.maximum(state[hh][0], cmax[hh]) for hh in hs]
        p = [jnp.exp2(s_ref[slot, hh] - m_new[hh]).astype(BF16) for hh in hs]
        alpha = [jnp.exp2(state[hh][0] - m_new[hh]) for hh in hs]
        vt = [jnp.concatenate([vt_ref[c, (0 if heads_share_kv else hh * dv):
                                      (dv if heads_share_kv else (hh + 1) * dv), :], ones], axis=0) for hh in hs]
        pv = [jnp.dot(vt[hh], p[hh], preferred_element_type=F32) for hh in hs]
        return tuple((m_new[hh], state[hh][1] * alpha[hh] + pv[hh]) for hh in hs)

    def body(i, carry):
        state, cmax = carry
        c = 2 * i
        cmax1 = scores(c + 1, 1)
        state = accumulate(c, 0, state, cmax)
        cmax0 = scores(jnp.minimum(c + 2, n_chunks - 1), 0)
        state = accumulate(c + 1, 1, state, cmax1)
        return state, cmax0

    state = tuple((jnp.full((1, tq), -jnp.inf, F32), jnp.zeros((dv + 16, tq), F32)) for _ in hs)
    cmax = scores(0, 0)
    if n_chunks > 1:
        assert n_chunks % 2 == 0
        state, _ = lax.fori_loop(0, n_chunks // 2, body, (state, cmax))
    else:
        state = accumulate(0, 0, state, cmax)
    outs = [acc[:dv] * (1.0 / acc[dv:dv + 1]) for _, acc in state]
    o_ref[...] = jnp.concatenate(outs, axis=0).T.astype(BF16)


def _attn_call(q, k, vt, batch, seq, tq, heads_share_kv, q_heads_per_kv):
    t = q.shape[0]
    n_heads = q.shape[1] // LANE
    tk = vt.shape[2]
    dv = 64
    n_chunks = seq // tk
    nq = seq // tq
    if heads_share_kv:
        k_spec = pl.BlockSpec((seq, LANE), lambda b, j, i: (b, 0))
        pairs_per_kv = q_heads_per_kv // 2
        vt_spec = pl.BlockSpec((n_chunks, dv, tk), lambda b, j, i: (b, j // pairs_per_kv, 0))
    else:
        k_spec = pl.BlockSpec((seq, 2 * LANE), lambda b, j, i: (b, j))
        vt_spec = pl.BlockSpec((n_chunks, 2 * dv, tk), lambda b, j, i: (b, j, 0))
    return pl.pallas_call(
        functools.partial(_attn_kernel, heads_share_kv=heads_share_kv, dv=dv),
        grid=(batch, n_heads // 2, nq),
        in_specs=[pl.BlockSpec((tq, 2 * LANE), lambda b, j, i: (b * nq + i, j)), k_spec, vt_spec],
        out_specs=pl.BlockSpec((tq, 2 * dv), lambda b, j, i: (b * nq + i, j)),
        out_shape=jax.ShapeDtypeStruct((t, n_heads * dv), BF16),
        compiler_params=_cparams(("arbitrary", "arbitrary", "arbitrary")),
        scratch_shapes=[pltpu.VMEM((2, 2, tk, tq), F32)],
        name="attn_shared_kv" if heads_share_kv else "attn_latent",
    )(q, k, vt)


def _merge_kernel(ya_ref, yb_ref, yc_ref, gates_ref, x_ref, mod_ref, n2g_ref, wpa_ref, wpb_ref,
                  wpc_ref, wout_ref, wq_ref, sk_ref,
                  x1_ref, h2_ref, lrow_ref, e1_ref, r2_ref, e2_ref, vals_ref, s_scr, r1_scr):
    d = D_MODEL
    merged = (gates_ref[:, 0:d].astype(F32) * jnp.dot(ya_ref[...], wpa_ref[...], preferred_element_type=F32)
              + gates_ref[:, d:2 * d].astype(F32) * jnp.dot(yb_ref[...], wpb_ref[...], preferred_element_type=F32)
              + gates_ref[:, 2 * d:3 * d].astype(F32) * jnp.dot(yc_ref[...], wpc_ref[...], preferred_element_type=F32))
    x1 = x_ref[...] + mod_ref[0, 2:3, :] * jnp.dot(merged.astype(BF16), wout_ref[...],
                                                   preferred_element_type=F32)
    x1_ref[...] = x1
    h2 = (_rms_rows(x1, n2g_ref[...]) * (1.0 + mod_ref[0, 4:5, :]) + mod_ref[0, 3:4, :]).astype(BF16)
    h2_ref[...] = h2
    qk = jnp.dot(h2, wq_ref[...], preferred_element_type=F32).astype(BF16)

    for h in range(PEER_HEADS):
        for p in range(2):
            idx = 2 * h + p
            s = lax.dot_general(sk_ref[idx], qk[:, idx * PEER_HALF:(idx + 1) * PEER_HALF],
                                (((1,), (1,)), ((), ())), preferred_element_type=F32)
            s_scr[p, h] = s
            for t0 in range(0, s.shape[1], LANE):
                st = s[:, t0:t0 + LANE]
                rank = jnp.full(st.shape, float(PEER_TOPK), F32)
                for a in range(PEER_TOPK):
                    mx = jnp.max(st, axis=0, keepdims=True)
                    vals_ref[p, a, h:h + 1, t0:t0 + LANE] = mx
                    hit = st == mx
                    rank = jnp.where(hit, float(a), rank)
                    st = jnp.where(hit, -jnp.inf, st)
                if p == 0:
                    r1_scr[h, :, t0:t0 + LANE] = rank
                else:
                    r2_ref[h, :, t0:t0 + LANE] = rank.astype(BF16)

    v1 = [vals_ref[0, a] for a in range(PEER_TOPK)]
    v2 = [vals_ref[1, a] for a in range(PEER_TOPK)]
    cands = [v1[a] + v2[b] for a, b in CAND_PAIRS]
    work = list(cands)
    tau = None
    for it in range(PEER_TOPK):
        mx = functools.reduce(jnp.maximum, work)
        if it == PEER_TOPK - 1:
            tau = mx
        else:
            work = [jnp.where(w == mx, -jnp.inf, w) for w in work]
    top = v1[0] + v2[0]
    zsum = functools.reduce(jnp.add, [jnp.where(c >= tau, jnp.exp(c - top), 0.0) for c in cands])
    rz = 1.0 / zsum
    count = [functools.reduce(jnp.add, [jnp.where(c >= tau, 1.0, 0.0)
                                         for c, (a2, _) in zip(cands, CAND_PAIRS) if a2 == a])
             for a in range(PEER_TOPK)]
    for h in range(PEER_HEADS):
        r1 = r1_scr[h]
        lrow = jnp.zeros(r1.shape, F32)
        for a in range(PEER_TOPK):
            lrow = jnp.where(r1 == float(a), count[a][h:h + 1, :], lrow)
        lrow_ref[h] = lrow
        e1_ref[h] = jnp.exp(s_scr[0, h] - v1[0][h:h + 1, :])
        e2_ref[h] = (jnp.exp(s_scr[1, h] - v2[0][h:h + 1, :]) * rz[h:h + 1, :]).astype(BF16)


def _merge_call(ya, yb, yc, gates, x, mod, lw, tm, seq):
    t, d = x.shape
    nb = t // tm
    per_seq = seq // tm
    row = lambda i: (i, 0)
    hk = pl.BlockSpec((PEER_HEADS, PEER_N_KEYS, tm), lambda i: (0, 0, i))
    hk_f32 = jax.ShapeDtypeStruct((PEER_HEADS, PEER_N_KEYS, t), F32)
    hk_bf16 = jax.ShapeDtypeStruct((PEER_HEADS, PEER_N_KEYS, t), BF16)
    return pl.pallas_call(
        _merge_kernel,
        grid=(nb,),
        in_specs=[
            pl.BlockSpec((tm, A_WIDTH), row),
            pl.BlockSpec((tm, A_WIDTH), row),
            pl.BlockSpec((tm, A_WIDTH), row),
            pl.BlockSpec((tm, N_BRANCH * d), row),
            pl.BlockSpec((tm, d), row),
            pl.BlockSpec((1, 6, d), lambda i: (i // per_seq, 0, 0)),
            _const_spec((1, d)),
            _const_spec((A_WIDTH, d)),
            _const_spec((A_WIDTH, d)),
            _const_spec((A_WIDTH, d)),
            _const_spec((d, d)),
            _const_spec((d, 2 * PEER_HEADS * PEER_HALF)),
            _const_spec((2 * PEER_HEADS, PEER_N_KEYS, PEER_HALF)),
        ],
        out_specs=(
            pl.BlockSpec((tm, d), row),
            pl.BlockSpec((tm, d), row),
            hk, hk, hk, hk,
        ),
        out_shape=(
            jax.ShapeDtypeStruct((t, d), F32),
            jax.ShapeDtypeStruct((t, d), BF16),
            hk_f32, hk_f32, hk_bf16, hk_bf16,
        ),
        scratch_shapes=[pltpu.VMEM((2, PEER_TOPK, PEER_HEADS, tm), F32),
                        pltpu.VMEM((2, PEER_HEADS, PEER_N_KEYS, tm), F32),
                        pltpu.VMEM((PEER_HEADS, PEER_N_KEYS, tm), F32)],
        compiler_params=_cparams(("arbitrary",)),
        name="merge_route",
    )(ya, yb, yc, gates, x, mod, lw["n2g"], lw["w_pa"], lw["w_pb"], lw["w_pc"], lw["w_out"],
      lw["w_query"], lw["sub_keys"])


def _expert_kernel(h2_ref, lrow_ref, e1_ref, r2_ref, e2_ref, u_ref, vt_ref, x1_ref, mod_ref,
                   fg_ref, o_ref, acc_ref, w_ref, *, final_norm):
    ch = pl.program_id(1)
    rows = lrow_ref.shape[1]
    tm = h2_ref.shape[0]
    pack_rows = 16

    @pl.when(ch == 0)
    def _():
        acc_ref[...] = jnp.zeros_like(acc_ref)

    def row_tile(ref, h, r):
        row = jnp.broadcast_to(ref[h, r:r + 1, :], (pack_rows, tm)).astype(BF16)
        return jnp.concatenate([row] * (PEER_N_KEYS // pack_rows), axis=0)

    zero = jnp.zeros((PEER_N_KEYS, tm), BF16)
    for r in range(rows):
        gate = None
        for h in range(PEER_HEADS):
            sel = jnp.where(r2_ref[h] < row_tile(lrow_ref, h, r), e2_ref[h], zero) * row_tile(e1_ref, h, r)
            gate = sel if gate is None else gate + sel
        rs = slice(r * PEER_N_KEYS, (r + 1) * PEER_N_KEYS)
        w_ref[rs, :] = gate
    act = lax.dot_general(u_ref[...], h2_ref[...], (((1,), (1,)), ((), ())), preferred_element_type=F32)
    for r in range(rows):
        rs = slice(r * PEER_N_KEYS, (r + 1) * PEER_N_KEYS)
        w_ref[rs, :] = w_ref[rs, :] * _gelu(act[rs, :].astype(BF16))
    acc_ref[...] += jnp.dot(vt_ref[...], w_ref[...], preferred_element_type=F32)

    @pl.when(ch == pl.num_programs(1) - 1)
    def _():
        x2 = x1_ref[...] + mod_ref[0, 5:6, :] * acc_ref[...].T
        if final_norm:
            x2 = _rms_rows(x2, fg_ref[...])
        o_ref[...] = x2


def _expert_call(h2, lrow, e1, r2, e2, u, vt, x1, mod, final_gain, tm, ec, seq, final_norm):
    t, d = x1.shape
    n_exp = u.shape[0]
    rows = ec // PEER_N_KEYS
    per_seq = seq // tm
    tok = lambda i, c: (i, 0)
    chunk_rows = pl.BlockSpec((PEER_HEADS, rows, tm), lambda i, c: (0, c, i))
    all_rows = pl.BlockSpec((PEER_HEADS, PEER_N_KEYS, tm), lambda i, c: (0, 0, i))
    return pl.pallas_call(
        functools.partial(_expert_kernel, final_norm=final_norm),
        grid=(t // tm, n_exp // ec),
        in_specs=[
            pl.BlockSpec((tm, d), tok),
            chunk_rows, chunk_rows, all_rows, all_rows,
            pl.BlockSpec((ec, d), lambda i, c: (c, 0)),
            pl.BlockSpec((d, ec), lambda i, c: (0, c)),
            pl.BlockSpec((tm, d), tok),
            pl.BlockSpec((1, 6, d), lambda i, c: (i // per_seq, 0, 0)),
            pl.BlockSpec((1, d), lambda i, c: (0, 0)),
        ],
        out_specs=pl.BlockSpec((tm, d), tok),
        out_shape=jax.ShapeDtypeStruct((t, d), F32),
        scratch_shapes=[pltpu.VMEM((d, tm), F32), pltpu.VMEM((ec, tm), BF16)],
        compiler_params=_cparams(("arbitrary", "arbitrary")),
        name="peer_dense",
    )(h2, lrow, e1, r2, e2, u, vt, x1, mod, final_gain)


def _w_in_column_map():
    src = np.full((W_IN_COLS,), -1, np.int64)
    src[OFF_A:OFF_A + COLS_A] = np.arange(COLS_A)
    half = B_HEAD_DIM // 2
    pair = np.concatenate([np.arange(0, B_HEAD_DIM, 2), np.arange(1, B_HEAD_DIM, 2)])
    tiles = B_HEADS * B_HEAD_DIM // LANE
    for t in range(tiles):
        for side, head in enumerate((t, tiles + t)):
            dst = OFF_B + t * LANE + side * B_HEAD_DIM
            src[dst:dst + B_HEAD_DIM] = COLS_A + head * B_HEAD_DIM + pair
    qw = B_HEADS * B_HEAD_DIM
    for kv in range(B_KV_HEADS):
        dst = OFF_B + qw + kv * B_HEAD_DIM
        src[dst:dst + B_HEAD_DIM] = COLS_A + qw + kv * B_HEAD_DIM + pair
    kw = B_KV_HEADS * B_HEAD_DIM
    src[OFF_B + qw + kw:OFF_B + qw + 2 * kw] = COLS_A + qw + kw + np.arange(kw)
    base_c = COLS_A + COLS_B
    src[OFF_C:OFF_C + C_Q_RANK + C_KV_RANK] = base_c + np.arange(C_Q_RANK + C_KV_RANK)
    rope_src = base_c + C_Q_RANK + C_KV_RANK
    dst = OFF_C + C_Q_RANK + C_KV_RANK + C_NOPE
    src[dst:dst + C_ROPE // 2] = rope_src + np.arange(0, C_ROPE, 2)
    src[dst + C_ROPE // 2:dst + C_ROPE] = rope_src + np.arange(1, C_ROPE, 2)
    src[OFF_G:] = base_c + COLS_C + np.arange(N_BRANCH * D_MODEL)
    del half
    return src


def _take_cols(w, src):
    mask = jnp.asarray(src >= 0, w.dtype)
    return jnp.take(w, jnp.asarray(np.maximum(src, 0)), axis=-1) * mask


def _uq_column_map():
    src = np.full((C_HEADS * LANE,), -1, np.int64)
    per = C_NOPE + C_ROPE
    for h in range(C_HEADS):
        src[h * LANE:h * LANE + C_NOPE] = h * per + np.arange(C_NOPE)
        src[h * LANE + C_NOPE:h * LANE + C_NOPE + C_ROPE // 2] = h * per + C_NOPE + np.arange(0, C_ROPE, 2)
        src[h * LANE + C_NOPE + C_ROPE // 2:h * LANE + per] = h * per + C_NOPE + np.arange(1, C_ROPE, 2)
    return src


def _ukv_column_map():
    src = np.full((C_HEADS * (LANE + C_V),), -1, np.int64)
    per = C_NOPE + C_V
    for h in range(C_HEADS):
        src[h * LANE:h * LANE + C_NOPE] = h * per + np.arange(C_NOPE)
        src[C_HEADS * LANE + h * C_V:C_HEADS * LANE + (h + 1) * C_V] = h * per + C_NOPE + np.arange(C_V)
    return src


def _rope_tables(seq):
    rows = seq // GRID_W
    row = jnp.repeat(jnp.arange(rows, dtype=F32), GRID_W)
    col = jnp.tile(jnp.arange(GRID_W, dtype=F32), rows)

    def cos_sin(d_rot):
        half = d_rot // 2
        freq = ROPE_THETA ** (-jnp.arange(0, half, 2, dtype=F32) / half)
        ang = jnp.concatenate([row[:, None] * freq, col[:, None] * freq], axis=-1)
        return jnp.cos(ang), jnp.sin(ang)

    cb, sb = cos_sin(B_HEAD_DIM)
    cosb = jnp.tile(jnp.concatenate([cb, cb], axis=1), (1, LANE // B_HEAD_DIM))
    sinb = jnp.tile(jnp.concatenate([-sb, sb], axis=1), (1, LANE // B_HEAD_DIM))
    cc, sc = cos_sin(C_ROPE)
    one = jnp.ones((seq, C_NOPE), F32)
    pad = LANE - C_NOPE - C_ROPE
    cosc = jnp.concatenate([one, cc, cc, jnp.ones((seq, pad), F32)], axis=1)
    sinc = jnp.concatenate([0.0 * one, -sc, sc, jnp.zeros((seq, pad), F32)], axis=1)
    return cosb, sinb, cosc, sinc


def kernel(x, c, w_ada, b_ada, norm1_gain, w_in, a_v_gain, a_w_s, a_b_s, b_q_gain, b_k_gain,
           c_q_gain, c_w_uq, c_kv_gain, c_w_ukv, w_pa, w_pb, w_pc, w_out, norm2_gain,
           peer_w_query, peer_sub_keys, peer_u, peer_v, final_gain):
    batch, seq, d = x.shape
    depth = w_ada.shape[0]
    t = batch * seq
    assert d == D_MODEL and seq % GRID_W == 0
    tm = min(512, seq)
    tq = min(1024, seq)
    tm_merge = min(256, seq)
    tm_peer = min(1024, seq)
    ec = 1024
    assert seq % tm == 0 and seq % tq == 0

    cosb, sinb, cosc, sinc = _rope_tables(seq)
    qw = B_HEADS * B_HEAD_DIM
    ones = jnp.asarray(np.kron(np.eye(qw // B_HEAD_DIM), np.ones((B_HEAD_DIM, B_HEAD_DIM))), BF16)
    tabs = dict(cosb=cosb, sinb=sinb, cosc=cosc, sinc=sinc, ones=ones)

    c_pad = jnp.zeros((8, d), F32).at[:batch].set(c)
    mod_all = _ada_call(c_pad, w_ada, b_ada)[:, :batch, :].reshape(depth, batch, 6, d)

    in_map, uq_map, ukv_map = _w_in_column_map(), _uq_column_map(), _ukv_column_map()
    pair = np.concatenate([np.arange(0, B_HEAD_DIM, 2), np.arange(1, B_HEAD_DIM, 2)])
    w_in_r = _take_cols(w_in, in_map).astype(BF16)
    w_uq_r = _take_cols(c_w_uq, uq_map).astype(BF16)
    w_ukv_r = _take_cols(c_w_ukv, ukv_map).astype(BF16)
    u_b = peer_u.astype(BF16)
    vt_b = jnp.swapaxes(peer_v, 1, 2).astype(BF16)

    xf = x.reshape(t, d)
    for l in range(depth):
        lw = dict(
            n1g=norm1_gain[l][None, :],
            w_in=w_in_r[l],
            a_v_gain=a_v_gain[l][None, :],
            w_s=a_w_s[l].astype(BF16),
            bmat=jnp.repeat(a_b_s[l].T, A_DIM, axis=1),
            qg=jnp.tile(b_q_gain[l][pair], B_HEADS)[None, :],
            kg=jnp.tile(b_k_gain[l][pair], B_KV_HEADS)[None, :],
            cqg=c_q_gain[l][None, :],
            w_uq=w_uq_r[l],
            ckvg=c_kv_gain[l][None, :],
            w_ukv=w_ukv_r[l],
            n2g=norm2_gain[l][None, :],
            w_pa=w_pa[l].astype(BF16),
            w_pb=w_pb[l].astype(BF16),
            w_pc=w_pc[l].astype(BF16),
            w_out=w_out[l].astype(BF16),
            w_query=peer_w_query[l].astype(BF16),
            sub_keys=peer_sub_keys[l].reshape(2 * PEER_HEADS, PEER_N_KEYS, PEER_HALF).astype(BF16),
        )
        mod = mod_all[l]
        ya, gates, qb, kb, vtb, qc, kc, vtc = _inproj_call(xf, mod, lw, tabs, tm, seq)
        yb = _attn_call(qb, kb, vtb, batch, seq, tq, True, B_HEADS // B_KV_HEADS)
        yc = _attn_call(qc, kc, vtc, batch, seq, tq, False, 1)
        x1, h2, lrow, e1, r2, e2 = _merge_call(ya, yb, yc, gates, xf, mod, lw, tm_merge, seq)
        xf = _expert_call(h2, lrow, e1, r2, e2, u_b[l], vt_b[l], x1, mod, final_gain[None, :],
                          tm_peer, ec, seq, l == depth - 1)
    return xf.reshape(batch, seq, d)
```

```python
import functools

import numpy as np
import jax
import jax.numpy as jnp
from jax import lax
from jax.experimental import pallas as pl
from jax.experimental.pallas import tpu as pltpu

F32 = jnp.float32
BF16 = jnp.bfloat16

LANE = 128
VMEM_LIMIT = 56 * 1024 * 1024

D_MODEL = 1024
GRID_W = 64
BLOCK = 128
EPS = 1e-6
ROPE_THETA = 10000.0
LOG2E = 1.4426950408889634

A_GROUPS = 8
A_DIM = 64
A_WIDTH = A_GROUPS * A_DIM
B_HEADS = 8
B_KV_HEADS = 2
B_HEAD_DIM = 64
C_HEADS = 8
C_NOPE = 64
C_ROPE = 32
C_V = 64
C_Q_RANK = 256
C_KV_RANK = 128
COLS_A = 2 * A_WIDTH
COLS_B = (B_HEADS + 2 * B_KV_HEADS) * B_HEAD_DIM
COLS_C = C_Q_RANK + C_KV_RANK + C_ROPE
N_BRANCH = 3
PEER_HEADS = 8
PEER_N_KEYS = 128
PEER_HALF = 128
PEER_TOPK = 16

OFF_A = 0
OFF_B = OFF_A + COLS_A
OFF_C = OFF_B + COLS_B
OFF_G = OFF_C + C_Q_RANK + C_KV_RANK + LANE
W_IN_COLS = OFF_G + N_BRANCH * D_MODEL

CAND_PAIRS = [(a, b) for a in range(PEER_TOPK) for b in range(PEER_TOPK) if (a + 1) * (b + 1) <= PEER_TOPK]


def _bitonic_merge_pairs(n):
    pairs, stride = [], n // 2
    while stride:
        pairs += [(i, i + stride) for i in range(n) if not i & stride]
        stride //= 2
    return pairs


def _sort_pairs(n):
    pairs = []

    def merge(lo, cnt, step):
        nxt = step * 2
        if nxt < cnt:
            merge(lo, cnt, nxt)
            merge(lo + step, cnt, nxt)
            pairs.extend((i, i + step) for i in range(lo + step, lo + cnt - step, nxt))
        else:
            pairs.append((lo, lo + step))

    def sort(lo, cnt):
        if cnt > 1:
            half = cnt // 2
            sort(lo, half)
            sort(lo + half, half)
            merge(lo, cnt, 1)

    sort(0, n)
    return pairs


SORT16 = _sort_pairs(PEER_TOPK)
BITONIC16 = _bitonic_merge_pairs(PEER_TOPK)


def _cparams(sem):
    return pltpu.CompilerParams(dimension_semantics=sem, vmem_limit_bytes=VMEM_LIMIT)


def _const_spec(shape):
    nd = len(shape)
    return pl.BlockSpec(shape, lambda *_: (0,) * nd, pipeline_mode=pl.Buffered(1))


def _layer_spec(stacked, layer):
    shape = stacked.shape[1:]
    return pl.BlockSpec((None,) + shape, lambda *_: (layer,) + (0,) * len(shape), pipeline_mode=pl.Buffered(1))


def _gelu(x):
    return 0.5 * x * (1.0 + jnp.tanh(0.7978845608028654 * (x + 0.044715 * (x * x * x))))


def _rms_rows(x, gain):
    return x * lax.rsqrt(jnp.mean(x * x, axis=-1, keepdims=True) + EPS) * gain


def _group_sumsq(x, ones_blockdiag):
    sq = x * x
    hi = sq.astype(BF16)
    lo = (sq - hi.astype(F32)).astype(BF16)
    return (jnp.dot(hi, ones_blockdiag, preferred_element_type=F32)
            + jnp.dot(lo, ones_blockdiag, preferred_element_type=F32))


def _swap_halves(x, group, lo_mask):
    w = x.shape[-1]
    half = group // 2
    return jnp.where(lo_mask, pltpu.roll(x, w - half, 1), pltpu.roll(x, half, 1))


def _ada_kernel(c_ref, w_ref, b_ref, o_ref):
    c = c_ref[...]
    ca = c * jax.nn.sigmoid(c)
    o_ref[0] = jnp.dot(ca, w_ref[0], preferred_element_type=F32,
                       precision=lax.Precision.HIGHEST) + b_ref[0]


def _ada_call(c_pad, w_ada, b_ada):
    depth, d, n = w_ada.shape
    tn = 1536
    return pl.pallas_call(
        _ada_kernel,
        grid=(depth, n // tn),
        in_specs=[
            pl.BlockSpec((8, d), lambda l, j: (0, 0)),
            pl.BlockSpec((1, d, tn), lambda l, j: (l, 0, j)),
            pl.BlockSpec((1, 1, tn), lambda l, j: (l, 0, j)),
        ],
        out_specs=pl.BlockSpec((1, 8, tn), lambda l, j: (l, 0, j)),
        out_shape=jax.ShapeDtypeStruct((depth, 8, n), F32),
        compiler_params=_cparams(("arbitrary", "arbitrary")),
        name="ada_mod",
    )(c_pad, w_ada, b_ada.reshape(depth, 1, n))


def _inproj_kernel(x_ref, mod_ref, n1g_ref, win_ref, avg_ref, ws_ref, bmat_ref, ones_ref,
                   qg_ref, kg_ref, cosb_ref, sinb_ref, cqg_ref, wuq_ref, ckvg_ref, wukv_ref,
                   cosc_ref, sinc_ref,
                   ya_ref, gates_ref, qb_ref, kb_ref, vtb_ref, qc_ref, kc_ref, vtc_ref):
    tm = x_ref.shape[0]
    d = D_MODEL
    shift = mod_ref[0, 0:1, :]
    scale = mod_ref[0, 1:2, :]
    h = (_rms_rows(x_ref[...], n1g_ref[...]) * (1.0 + scale) + shift).astype(BF16)

    za = jnp.dot(h, win_ref[:, OFF_A:OFF_A + COLS_A], preferred_element_type=F32)
    z = _gelu(za)
    u = z[:, :A_WIDTH]
    vn = _rms_rows(z[:, A_WIDTH:], avg_ref[...]).astype(BF16)
    grp = lax.broadcasted_iota(jnp.int32, (BLOCK, A_WIDTH), 1) // A_DIM
    for c in range(tm // BLOCK):
        vc = vn[c * BLOCK:(c + 1) * BLOCK]
        mixed = jnp.zeros((BLOCK, A_WIDTH), F32)
        for g in range(A_GROUPS):
            r = jnp.dot(ws_ref[g], vc, preferred_element_type=F32)
            mixed = jnp.where(grp == g, r, mixed)
        ya_ref[c * BLOCK:(c + 1) * BLOCK, :] = (
            u[c * BLOCK:(c + 1) * BLOCK] * (mixed + bmat_ref[...])).astype(BF16)

    zb = jnp.dot(h, win_ref[:, OFF_B:OFF_B + COLS_B], preferred_element_type=F32)
    qw = B_HEADS * B_HEAD_DIM
    kw = B_KV_HEADS * B_HEAD_DIM
    cosb = cosb_ref[...]
    sinb = sinb_ref[...]
    lane_q = lax.broadcasted_iota(jnp.int32, (tm, qw), 1)
    q = zb[:, :qw]
    qn = q * lax.rsqrt(_group_sumsq(q, ones_ref[...]) * (1.0 / B_HEAD_DIM) + EPS) * qg_ref[...]
    reps = qw // LANE
    qr = (qn * jnp.concatenate([cosb] * reps, axis=1)
          + _swap_halves(qn, B_HEAD_DIM, (lane_q % B_HEAD_DIM) < B_HEAD_DIM // 2)
          * jnp.concatenate([sinb] * reps, axis=1)) * (B_HEAD_DIM ** -0.5 * LOG2E)
    lane_t = lax.broadcasted_iota(jnp.int32, (tm, LANE), 1)
    for t in range(reps):
        tile = qr[:, t * LANE:(t + 1) * LANE]
        qb_ref[:, t * LANE:(t + 1) * LANE] = jnp.where(lane_t < B_HEAD_DIM, tile, 0.0).astype(BF16)
        qb_ref[:, (reps + t) * LANE:(reps + t + 1) * LANE] = (
            jnp.where(lane_t >= B_HEAD_DIM, tile, 0.0).astype(BF16))
    k = zb[:, qw:qw + kw]
    kn = k * lax.rsqrt(_group_sumsq(k, ones_ref[0:kw, 0:kw]) * (1.0 / B_HEAD_DIM) + EPS) * kg_ref[...]
    kr = kn * cosb + _swap_halves(kn, B_HEAD_DIM, (lane_t % B_HEAD_DIM) < B_HEAD_DIM // 2) * sinb
    kb_ref[...] = kr.astype(BF16)
    vtb_ref[0] = zb[:, qw + kw:].T.astype(BF16)

    zc = jnp.dot(h, win_ref[:, OFF_C:OFF_G], preferred_element_type=F32)
    cosc = cosc_ref[...]
    sinc = sinc_ref[...]
    rope_lo = C_NOPE + C_ROPE // 2
    cqn = _rms_rows(zc[:, :C_Q_RANK], cqg_ref[...]).astype(BF16)
    qf = jnp.dot(cqn, wuq_ref[...], preferred_element_type=F32)
    lane_c = lax.broadcasted_iota(jnp.int32, qf.shape, 1)
    qrot = (qf * jnp.concatenate([cosc] * C_HEADS, axis=1)
            + _swap_halves(qf, C_ROPE, (lane_c % LANE) < rope_lo)
            * jnp.concatenate([sinc] * C_HEADS, axis=1)) * ((C_NOPE + C_ROPE) ** -0.5 * LOG2E)
    qc_ref[...] = qrot.astype(BF16)
    ckvn = _rms_rows(zc[:, C_Q_RANK:C_Q_RANK + C_KV_RANK], ckvg_ref[...]).astype(BF16)
    kv = jnp.dot(ckvn, wukv_ref[...], preferred_element_type=F32)
    krope = zc[:, C_Q_RANK + C_KV_RANK:]
    krot = krope * cosc + _swap_halves(krope, C_ROPE, lane_t < rope_lo) * sinc
    kc_ref[...] = (kv[:, :C_HEADS * LANE] + jnp.concatenate([krot] * C_HEADS, axis=1)).astype(BF16)
    vtc_ref[0] = kv[:, C_HEADS * LANE:].T.astype(BF16)

    for n in range(N_BRANCH):
        zg = jnp.dot(h, win_ref[:, OFF_G + n * d:OFF_G + (n + 1) * d], preferred_element_type=F32)
        gates_ref[:, n * d:(n + 1) * d] = jax.nn.sigmoid(zg).astype(BF16)


def _inproj_call(x, mod, lw, tabs, layer, tm, seq):
    t, d = x.shape
    nb = t // tm
    per_seq = seq // tm
    row = lambda i: (i, 0)
    pos = lambda i: (i % per_seq, 0)
    per_layer = lambda name: _layer_spec(lw[name], layer)
    in_specs = [
        pl.BlockSpec((tm, d), row),
        pl.BlockSpec((None, 1, 6, d), lambda i: (layer, i // per_seq, 0, 0)),
        per_layer("n1g"),
        per_layer("w_in"),
        per_layer("a_v_gain"),
        per_layer("w_s"),
        per_layer("bmat"),
        _const_spec((B_HEADS * B_HEAD_DIM, B_HEADS * B_HEAD_DIM)),
        per_layer("qg"),
        per_layer("kg"),
        pl.BlockSpec((tm, LANE), pos),
        pl.BlockSpec((tm, LANE), pos),
        per_layer("cqg"),
        per_layer("w_uq"),
        per_layer("ckvg"),
        per_layer("w_ukv"),
        pl.BlockSpec((tm, LANE), pos),
        pl.BlockSpec((tm, LANE), pos),
    ]
    out_shapes = (
        jax.ShapeDtypeStruct((t, A_WIDTH), BF16),
        jax.ShapeDtypeStruct((t, N_BRANCH * d), BF16),
        jax.ShapeDtypeStruct((t, B_HEADS * LANE), BF16),
        jax.ShapeDtypeStruct((t, LANE), BF16),
        jax.ShapeDtypeStruct((nb, B_KV_HEADS * B_HEAD_DIM, tm), BF16),
        jax.ShapeDtypeStruct((t, C_HEADS * LANE), BF16),
        jax.ShapeDtypeStruct((t, C_HEADS * LANE), BF16),
        jax.ShapeDtypeStruct((nb, C_HEADS * C_V, tm), BF16),
    )
    out_specs = (
        pl.BlockSpec((tm, A_WIDTH), row),
        pl.BlockSpec((tm, N_BRANCH * d), row),
        pl.BlockSpec((tm, B_HEADS * LANE), row),
        pl.BlockSpec((tm, LANE), row),
        pl.BlockSpec((1, B_KV_HEADS * B_HEAD_DIM, tm), lambda i: (i, 0, 0)),
        pl.BlockSpec((tm, C_HEADS * LANE), row),
        pl.BlockSpec((tm, C_HEADS * LANE), row),
        pl.BlockSpec((1, C_HEADS * C_V, tm), lambda i: (i, 0, 0)),
    )
    return pl.pallas_call(
        _inproj_kernel,
        grid=(nb,),
        in_specs=in_specs,
        out_specs=out_specs,
        out_shape=out_shapes,
        compiler_params=_cparams(("arbitrary",)),
        name="in_proj",
    )(x, mod, lw["n1g"], lw["w_in"], lw["a_v_gain"], lw["w_s"], lw["bmat"], tabs["ones"],
      lw["qg"], lw["kg"], tabs["cosb"], tabs["sinb"], lw["cqg"], lw["w_uq"], lw["ckvg"], lw["w_ukv"],
      tabs["cosc"], tabs["sinc"])


def _attn_kernel(q_ref, k_ref, vt_ref, o_ref, s_ref, *, heads_share_kv, dv):
    tq = q_ref.shape[0]
    n_chunks, _, tk = vt_ref.shape
    ones = jnp.ones((16, tk), BF16)
    hs = range(2)
    qs = [q_ref[:, hh * LANE:(hh + 1) * LANE] for hh in hs]

    def scores(c, slot):
        start = pl.multiple_of(c * tk, tk)
        s_t = [lax.dot_general(k_ref[pl.ds(start, tk), (0 if heads_share_kv else hh * LANE):
                                     (LANE if heads_share_kv else (hh + 1) * LANE)],
                               qs[hh], (((1,), (1,)), ((), ())), preferred_element_type=F32) for hh in hs]
        for hh in hs:
            s_ref[slot, hh] = s_t[hh]
        return tuple(jnp.max(s_t[hh], axis=0, keepdims=True) for hh in hs)

    def accumulate(c, slot, state, cmax):
        m_new = [jnp.maximum(state[hh][0], cmax[hh]) for hh in hs]
        p = [jnp.exp2(s_ref[slot, hh] - m_new[hh]).astype(BF16) for hh in hs]
        alpha = [jnp.exp2(state[hh][0] - m_new[hh]) for hh in hs]
        vt = [jnp.concatenate([vt_ref[c, (0 if heads_share_kv else hh * dv):
                                      (dv if heads_share_kv else (hh + 1) * dv), :], ones], axis=0) for hh in hs]
        pv = [jnp.dot(vt[hh], p[hh], preferred_element_type=F32) for hh in hs]
        return tuple((m_new[hh], state[hh][1] * alpha[hh] + pv[hh]) for hh in hs)

    def body(i, carry):
        state, cmax = carry
        c = 2 * i
        cmax1 = scores(c + 1, 1)
        state = accumulate(c, 0, state, cmax)
        cmax0 = scores(jnp.minimum(c + 2, n_chunks - 1), 0)
        state = accumulate(c + 1, 1, state, cmax1)
        return state, cmax0

    state = tuple((jnp.full((1, tq), -jnp.inf, F32), jnp.zeros((dv + 16, tq), F32)) for _ in hs)
    cmax = scores(0, 0)
    if n_chunks > 1:
        assert n_chunks % 2 == 0
        state, _ = lax.fori_loop(0, n_chunks // 2, body, (state, cmax))
    else:
        state = accumulate(0, 0, state, cmax)
    outs = [acc[:dv] * (1.0 / acc[dv:dv + 1]) for _, acc in state]
    o_ref[...] = jnp.concatenate(outs, axis=0).T.astype(BF16)


def _attn_call(q, k, vt, batch, seq, tq, heads_share_kv, q_heads_per_kv):
    t = q.shape[0]
    n_heads = q.shape[1] // LANE
    tk = vt.shape[2]
    dv = 64
    n_chunks = seq // tk
    nq = seq // tq
    if heads_share_kv:
        k_spec = pl.BlockSpec((seq, LANE), lambda b, j, i: (b, 0))
        pairs_per_kv = q_heads_per_kv // 2
        vt_spec = pl.BlockSpec((n_chunks, dv, tk), lambda b, j, i: (b, j // pairs_per_kv, 0))
    else:
        k_spec = pl.BlockSpec((seq, 2 * LANE), lambda b, j, i: (b, j))
        vt_spec = pl.BlockSpec((n_chunks, 2 * dv, tk), lambda b, j, i: (b, j, 0))
    return pl.pallas_call(
        functools.partial(_attn_kernel, heads_share_kv=heads_share_kv, dv=dv),
        grid=(batch, n_heads // 2, nq),
        in_specs=[pl.BlockSpec((tq, 2 * LANE), lambda b, j, i: (b * nq + i, j)), k_spec, vt_spec],
        out_specs=pl.BlockSpec((tq, 2 * dv), lambda b, j, i: (b * nq + i, j)),
        out_shape=jax.ShapeDtypeStruct((t, n_heads * dv), BF16),
        compiler_params=_cparams(("arbitrary", "arbitrary", "arbitrary")),
        scratch_shapes=[pltpu.VMEM((2, 2, tk, tq), F32)],
        name="attn_shared_kv" if heads_share_kv else "attn_latent",
    )(q, k, vt)


def _merge_kernel(ya_ref, yb_ref, yc_ref, gates_ref, x_ref, mod_ref, n2g_ref, wpa_ref, wpb_ref,
                  wpc_ref, wout_ref, wq_ref, sk_ref,
                  x1_ref, h2_ref, lrow_ref, e1_ref, r2_ref, e2_ref, vals_ref, s_scr):
    d = D_MODEL
    merged = (gates_ref[:, 0:d].astype(F32) * jnp.dot(ya_ref[...], wpa_ref[...], preferred_element_type=F32)
              + gates_ref[:, d:2 * d].astype(F32) * jnp.dot(yb_ref[...], wpb_ref[...], preferred_element_type=F32)
              + gates_ref[:, 2 * d:3 * d].astype(F32) * jnp.dot(yc_ref[...], wpc_ref[...], preferred_element_type=F32))
    x1 = x_ref[...] + mod_ref[0, 2:3, :] * jnp.dot(merged.astype(BF16), wout_ref[...],
                                                   preferred_element_type=F32)
    x1_ref[...] = x1
    h2 = (_rms_rows(x1, n2g_ref[...]) * (1.0 + mod_ref[0, 4:5, :]) + mod_ref[0, 3:4, :]).astype(BF16)
    h2_ref[...] = h2
    qk = jnp.dot(h2, wq_ref[...], preferred_element_type=F32).astype(BF16)

    sub = 8
    n_tiles = PEER_N_KEYS // sub
    assert n_tiles == PEER_TOPK
    for h in range(PEER_HEADS):
        for p in range(2):
            idx = 2 * h + p
            s = lax.dot_general(sk_ref[idx], qk[:, idx * PEER_HALF:(idx + 1) * PEER_HALF],
                                (((1,), (1,)), ((), ())), preferred_element_type=F32)
            s_scr[p, h] = s
            for t0 in range(0, s.shape[1], LANE):
                tiles = [s[k * sub:(k + 1) * sub, t0:t0 + LANE] for k in range(n_tiles)]
                top = list(tiles)
                for i, j in SORT16:
                    top[i], top[j] = jnp.maximum(top[i], top[j]), jnp.minimum(top[i], top[j])
                for shift in (4, 2, 1):
                    other = [pltpu.roll(v, shift, 0) for v in top]
                    top = [jnp.maximum(top[k], other[n_tiles - 1 - k]) for k in range(n_tiles)]
                    for i, j in BITONIC16:
                        top[i], top[j] = jnp.maximum(top[i], top[j]), jnp.minimum(top[i], top[j])
                for a in range(PEER_TOPK):
                    vals_ref[p, a, h:h + 1, t0:t0 + LANE] = top[a][0:1, :]
                if p == 1:
                    for k in range(0, n_tiles, 2):
                        rank = [functools.reduce(jnp.add, [jnp.where(tiles[k + i] < top[a], 1.0, 0.0)
                                                           for a in range(PEER_TOPK)]) for i in range(2)]
                        r2_ref[h, k * sub:(k + 2) * sub, t0:t0 + LANE] = jnp.concatenate(rank, axis=0).astype(BF16)

    v1 = [vals_ref[0, a] for a in range(PEER_TOPK)]
    v2 = [vals_ref[1, a] for a in range(PEER_TOPK)]
    cands = [v1[a] + v2[b] for a, b in CAND_PAIRS]
    work = list(cands)
    tau = None
    for it in range(PEER_TOPK):
        mx = functools.reduce(jnp.maximum, work)
        if it == PEER_TOPK - 1:
            tau = mx
        else:
            work = [jnp.where(w == mx, -jnp.inf, w) for w in work]
    top = v1[0] + v2[0]
    zsum = functools.reduce(jnp.add, [jnp.where(c >= tau, jnp.exp(c - top), 0.0) for c in cands])
    rz = 1.0 / zsum
    count = [functools.reduce(jnp.add, [jnp.where(c >= tau, 1.0, 0.0)
                                         for c, (a2, _) in zip(cands, CAND_PAIRS) if a2 == a])
             for a in range(PEER_TOPK)]
    for h in range(PEER_HEADS):
        s1 = s_scr[0, h]
        lrow = jnp.zeros(s1.shape, F32)
        for a in range(PEER_TOPK):
            lrow = jnp.where(s1 == v1[a][h:h + 1, :], count[a][h:h + 1, :], lrow)
        lrow_ref[h] = lrow
        e1_ref[h] = jnp.exp(s1 - v1[0][h:h + 1, :])
        e2_ref[h] = (jnp.exp(s_scr[1, h] - v2[0][h:h + 1, :]) * rz[h:h + 1, :]).astype(BF16)


def _merge_call(ya, yb, yc, gates, x, mod, lw, layer, tm, seq):
    t, d = x.shape
    nb = t // tm
    per_seq = seq // tm
    row = lambda i: (i, 0)
    per_layer = lambda name: _layer_spec(lw[name], layer)
    hk = pl.BlockSpec((PEER_HEADS, PEER_N_KEYS, tm), lambda i: (0, 0, i))
    hk_f32 = jax.ShapeDtypeStruct((PEER_HEADS, PEER_N_KEYS, t), F32)
    hk_bf16 = jax.ShapeDtypeStruct((PEER_HEADS, PEER_N_KEYS, t), BF16)
    return pl.pallas_call(
        _merge_kernel,
        grid=(nb,),
        in_specs=[
            pl.BlockSpec((tm, A_WIDTH), row),
            pl.BlockSpec((tm, A_WIDTH), row),
            pl.BlockSpec((tm, A_WIDTH), row),
            pl.BlockSpec((tm, N_BRANCH * d), row),
            pl.BlockSpec((tm, d), row),
            pl.BlockSpec((None, 1, 6, d), lambda i: (layer, i // per_seq, 0, 0)),
            per_layer("n2g"),
            per_layer("w_pa"),
            per_layer("w_pb"),
            per_layer("w_pc"),
            per_layer("w_out"),
            per_layer("w_query"),
            per_layer("sub_keys"),
        ],
        out_specs=(
            pl.BlockSpec((tm, d), row),
            pl.BlockSpec((tm, d), row),
            hk, hk, hk, hk,
        ),
        out_shape=(
            jax.ShapeDtypeStruct((t, d), F32),
            jax.ShapeDtypeStruct((t, d), BF16),
            hk_f32, hk_f32, hk_bf16, hk_bf16,
        ),
        scratch_shapes=[pltpu.VMEM((2, PEER_TOPK, PEER_HEADS, tm), F32),
                        pltpu.VMEM((2, PEER_HEADS, PEER_N_KEYS, tm), F32)],
        compiler_params=_cparams(("arbitrary",)),
        name="merge_route",
    )(ya, yb, yc, gates, x, mod, lw["n2g"], lw["w_pa"], lw["w_pb"], lw["w_pc"], lw["w_out"],
      lw["w_query"], lw["sub_keys"])


def _expert_kernel(h2_ref, lrow_ref, e1_ref, r2_ref, e2_ref, u_ref, vt_ref, x1_ref, mod_ref,
                   fg_ref, o_ref, acc_ref, w_ref, *, final_norm):
    ch = pl.program_id(1)
    rows = lrow_ref.shape[1]
    tm = h2_ref.shape[0]
    pack_rows = 16

    @pl.when(ch == 0)
    def _():
        acc_ref[...] = jnp.zeros_like(acc_ref)

    def row_tile(ref, h, r):
        row = jnp.broadcast_to(ref[h, r:r + 1, :], (pack_rows, tm)).astype(BF16)
        return jnp.concatenate([row] * (PEER_N_KEYS // pack_rows), axis=0)

    zero = jnp.zeros((PEER_N_KEYS, tm), BF16)
    for r in range(rows):
        gate = None
        for h in range(PEER_HEADS):
            sel = jnp.where(r2_ref[h] < row_tile(lrow_ref, h, r), e2_ref[h], zero) * row_tile(e1_ref, h, r)
            gate = sel if gate is None else gate + sel
        rs = slice(r * PEER_N_KEYS, (r + 1) * PEER_N_KEYS)
        w_ref[rs, :] = gate
    act = lax.dot_general(u_ref[...], h2_ref[...], (((1,), (1,)), ((), ())), preferred_element_type=F32)
    for r in range(rows):
        rs = slice(r * PEER_N_KEYS, (r + 1) * PEER_N_KEYS)
        w_ref[rs, :] = w_ref[rs, :] * _gelu(act[rs, :].astype(BF16))
    acc_ref[...] += jnp.dot(vt_ref[...], w_ref[...], preferred_element_type=F32)

    @pl.when(ch == pl.num_programs(1) - 1)
    def _():
        x2 = x1_ref[...] + mod_ref[0, 5:6, :] * acc_ref[...].T
        if final_norm:
            x2 = _rms_rows(x2, fg_ref[...])
        o_ref[...] = x2


def _expert_call(h2, lrow, e1, r2, e2, u, vt, x1, mod, final_gain, layer, tm, ec, seq, final_norm):
    t, d = x1.shape
    n_exp = u.shape[1]
    rows = ec // PEER_N_KEYS
    per_seq = seq // tm
    tok = lambda i, c: (i, 0)
    chunk_rows = pl.BlockSpec((PEER_HEADS, rows, tm), lambda i, c: (0, c, i))
    all_rows = pl.BlockSpec((PEER_HEADS, PEER_N_KEYS, tm), lambda i, c: (0, 0, i))
    return pl.pallas_call(
        functools.partial(_expert_kernel, final_norm=final_norm),
        grid=(t // tm, n_exp // ec),
        in_specs=[
            pl.BlockSpec((tm, d), tok),
            chunk_rows, chunk_rows, all_rows, all_rows,
            pl.BlockSpec((None, ec, d), lambda i, c: (layer, c, 0)),
            pl.BlockSpec((None, d, ec), lambda i, c: (layer, 0, c)),
            pl.BlockSpec((tm, d), tok),
            pl.BlockSpec((None, 1, 6, d), lambda i, c: (layer, i // per_seq, 0, 0)),
            pl.BlockSpec((1, d), lambda i, c: (0, 0)),
        ],
        out_specs=pl.BlockSpec((tm, d), tok),
        out_shape=jax.ShapeDtypeStruct((t, d), F32),
        scratch_shapes=[pltpu.VMEM((d, tm), F32), pltpu.VMEM((ec, tm), BF16)],
        compiler_params=_cparams(("arbitrary", "arbitrary")),
        name="peer_dense",
    )(h2, lrow, e1, r2, e2, u, vt, x1, mod, final_gain)


def _w_in_column_map():
    src = np.full((W_IN_COLS,), -1, np.int64)
    src[OFF_A:OFF_A + COLS_A] = np.arange(COLS_A)
    pair = np.concatenate([np.arange(0, B_HEAD_DIM, 2), np.arange(1, B_HEAD_DIM, 2)])
    tiles = B_HEADS * B_HEAD_DIM // LANE
    for t in range(tiles):
        for side, head in enumerate((t, tiles + t)):
            dst = OFF_B + t * LANE + side * B_HEAD_DIM
            src[dst:dst + B_HEAD_DIM] = COLS_A + head * B_HEAD_DIM + pair
    qw = B_HEADS * B_HEAD_DIM
    for kv in range(B_KV_HEADS):
        dst = OFF_B + qw + kv * B_HEAD_DIM
        src[dst:dst + B_HEAD_DIM] = COLS_A + qw + kv * B_HEAD_DIM + pair
    kw = B_KV_HEADS * B_HEAD_DIM
    src[OFF_B + qw + kw:OFF_B + qw + 2 * kw] = COLS_A + qw + kw + np.arange(kw)
    base_c = COLS_A + COLS_B
    src[OFF_C:OFF_C + C_Q_RANK + C_KV_RANK] = base_c + np.arange(C_Q_RANK + C_KV_RANK)
    rope_src = base_c + C_Q_RANK + C_KV_RANK
    dst = OFF_C + C_Q_RANK + C_KV_RANK + C_NOPE
    src[dst:dst + C_ROPE // 2] = rope_src + np.arange(0, C_ROPE, 2)
    src[dst + C_ROPE // 2:dst + C_ROPE] = rope_src + np.arange(1, C_ROPE, 2)
    src[OFF_G:] = base_c + COLS_C + np.arange(N_BRANCH * D_MODEL)
    return src


def _take_cols(w, src):
    parts, i, n = [], 0, len(src)
    while i < n:
        j = i + 1
        if src[i] < 0:
            while j < n and src[j] < 0:
                j += 1
            parts.append(jnp.zeros(w.shape[:-1] + (j - i,), w.dtype))
        else:
            step = int(src[j] - src[i]) if j < n and src[j] > src[i] else 1
            while j < n and src[j] == src[i] + (j - i) * step:
                j += 1
            parts.append(lax.slice_in_dim(w, int(src[i]), int(src[j - 1]) + 1, stride=step, axis=w.ndim - 1))
        i = j
    return jnp.concatenate(parts, axis=-1)


def _uq_column_map():
    src = np.full((C_HEADS * LANE,), -1, np.int64)
    per = C_NOPE + C_ROPE
    for h in range(C_HEADS):
        src[h * LANE:h * LANE + C_NOPE] = h * per + np.arange(C_NOPE)
        src[h * LANE + C_NOPE:h * LANE + C_NOPE + C_ROPE // 2] = h * per + C_NOPE + np.arange(0, C_ROPE, 2)
        src[h * LANE + C_NOPE + C_ROPE // 2:h * LANE + per] = h * per + C_NOPE + np.arange(1, C_ROPE, 2)
    return src


def _ukv_column_map():
    src = np.full((C_HEADS * (LANE + C_V),), -1, np.int64)
    per = C_NOPE + C_V
    for h in range(C_HEADS):
        src[h * LANE:h * LANE + C_NOPE] = h * per + np.arange(C_NOPE)
        src[C_HEADS * LANE + h * C_V:C_HEADS * LANE + (h + 1) * C_V] = h * per + C_NOPE + np.arange(C_V)
    return src


def _rope_tables(seq):
    rows = seq // GRID_W
    row = jnp.repeat(jnp.arange(rows, dtype=F32), GRID_W)
    col = jnp.tile(jnp.arange(GRID_W, dtype=F32), rows)

    def cos_sin(d_rot):
        half = d_rot // 2
        freq = ROPE_THETA ** (-jnp.arange(0, half, 2, dtype=F32) / half)
        ang = jnp.concatenate([row[:, None] * freq, col[:, None] * freq], axis=-1)
        return jnp.cos(ang), jnp.sin(ang)

    cb, sb = cos_sin(B_HEAD_DIM)
    cosb = jnp.tile(jnp.concatenate([cb, cb], axis=1), (1, LANE // B_HEAD_DIM))
    sinb = jnp.tile(jnp.concatenate([-sb, sb], axis=1), (1, LANE // B_HEAD_DIM))
    cc, sc = cos_sin(C_ROPE)
    one = jnp.ones((seq, C_NOPE), F32)
    pad = LANE - C_NOPE - C_ROPE
    cosc = jnp.concatenate([one, cc, cc, jnp.ones((seq, pad), F32)], axis=1)
    sinc = jnp.concatenate([0.0 * one, -sc, sc, jnp.zeros((seq, pad), F32)], axis=1)
    return cosb, sinb, cosc, sinc


def kernel(x, c, w_ada, b_ada, norm1_gain, w_in, a_v_gain, a_w_s, a_b_s, b_q_gain, b_k_gain,
           c_q_gain, c_w_uq, c_kv_gain, c_w_ukv, w_pa, w_pb, w_pc, w_out, norm2_gain,
           peer_w_query, peer_sub_keys, peer_u, peer_v, final_gain):
    batch, seq, d = x.shape
    depth = w_ada.shape[0]
    t = batch * seq
    assert d == D_MODEL and seq % GRID_W == 0
    tm = min(512, seq)
    tq = min(1024, seq)
    tm_merge = min(256, seq)
    tm_peer = min(1024, seq)
    ec = 1024
    assert seq % tm == 0 and seq % tq == 0

    cosb, sinb, cosc, sinc = _rope_tables(seq)
    qw = B_HEADS * B_HEAD_DIM
    ones = jnp.asarray(np.kron(np.eye(qw // B_HEAD_DIM), np.ones((B_HEAD_DIM, B_HEAD_DIM))), BF16)
    tabs = dict(cosb=cosb, sinb=sinb, cosc=cosc, sinc=sinc, ones=ones)

    c_pad = jnp.zeros((8, d), F32).at[:batch].set(c)
    mod_all = _ada_call(c_pad, w_ada, b_ada)[:, :batch, :].reshape(depth, batch, 6, d)

    pair = np.concatenate([np.arange(0, B_HEAD_DIM, 2), np.arange(1, B_HEAD_DIM, 2)])
    lw = dict(
        n1g=norm1_gain[:, None, :],
        w_in=_take_cols(w_in, _w_in_column_map()).astype(BF16),
        a_v_gain=a_v_gain[:, None, :],
        w_s=a_w_s.astype(BF16),
        bmat=jnp.repeat(jnp.swapaxes(a_b_s, 1, 2), A_DIM, axis=2),
        qg=jnp.tile(b_q_gain[:, pair], (1, B_HEADS))[:, None, :],
        kg=jnp.tile(b_k_gain[:, pair], (1, B_KV_HEADS))[:, None, :],
        cqg=c_q_gain[:, None, :],
        w_uq=_take_cols(c_w_uq, _uq_column_map()).astype(BF16),
        ckvg=c_kv_gain[:, None, :],
        w_ukv=_take_cols(c_w_ukv, _ukv_column_map()).astype(BF16),
        n2g=norm2_gain[:, None, :],
        w_pa=w_pa.astype(BF16),
        w_pb=w_pb.astype(BF16),
        w_pc=w_pc.astype(BF16),
        w_out=w_out.astype(BF16),
        w_query=peer_w_query.astype(BF16),
        sub_keys=peer_sub_keys.reshape(depth, 2 * PEER_HEADS, PEER_N_KEYS, PEER_HALF).astype(BF16),
    )
    u_b = peer_u.astype(BF16)
    vt_b = jnp.swapaxes(peer_v, 1, 2).astype(BF16)

    xf = x.reshape(t, d)
    for l in range(depth):
        ya, gates, qb, kb, vtb, qc, kc, vtc = _inproj_call(xf, mod_all, lw, tabs, l, tm, seq)
        yb = _attn_call(qb, kb, vtb, batch, seq, tq, True, B_HEADS // B_KV_HEADS)
        yc = _attn_call(qc, kc, vtc, batch, seq, tq, False, 1)
        x1, h2, lrow, e1, r2, e2 = _merge_call(ya, yb, yc, gates, xf, mod_all, lw, l, tm_merge, seq)
        xf = _expert_call(h2, lrow, e1, r2, e2, u_b, vt_b, x1, mod_all, final_gain[None, :],
                          l, tm_peer, ec, seq, l == depth - 1)
    return xf.reshape(batch, seq, d)
```

```python
import functools

import numpy as np
import jax
import jax.numpy as jnp
from jax import lax
from jax.experimental import pallas as pl
from jax.experimental.pallas import tpu as pltpu

F32 = jnp.float32
BF16 = jnp.bfloat16

LANE = 128
VMEM_LIMIT = 56 * 1024 * 1024

D_MODEL = 1024
GRID_W = 64
BLOCK = 128
EPS = 1e-6
ROPE_THETA = 10000.0
LOG2E = 1.4426950408889634

A_GROUPS = 8
A_DIM = 64
A_WIDTH = A_GROUPS * A_DIM
B_HEADS = 8
B_KV_HEADS = 2
B_HEAD_DIM = 64
C_HEADS = 8
C_NOPE = 64
C_ROPE = 32
C_V = 64
C_Q_RANK = 256
C_KV_RANK = 128
COLS_A = 2 * A_WIDTH
COLS_B = (B_HEADS + 2 * B_KV_HEADS) * B_HEAD_DIM
COLS_C = C_Q_RANK + C_KV_RANK + C_ROPE
N_BRANCH = 3
PEER_HEADS = 8
PEER_N_KEYS = 128
PEER_HALF = 128
PEER_TOPK = 16

OFF_A = 0
OFF_B = OFF_A + COLS_A
OFF_C = OFF_B + COLS_B
OFF_G = OFF_C + C_Q_RANK + C_KV_RANK + LANE
W_IN_COLS = OFF_G + N_BRANCH * D_MODEL

CAND_PAIRS = [(a, b) for a in range(PEER_TOPK) for b in range(PEER_TOPK) if (a + 1) * (b + 1) <= PEER_TOPK]


def _bitonic_merge_pairs(n):
    pairs, stride = [], n // 2
    while stride:
        pairs += [(i, i + stride) for i in range(n) if not i & stride]
        stride //= 2
    return pairs


def _sort_pairs(n):
    pairs = []

    def merge(lo, cnt, step):
        nxt = step * 2
        if nxt < cnt:
            merge(lo, cnt, nxt)
            merge(lo + step, cnt, nxt)
            pairs.extend((i, i + step) for i in range(lo + step, lo + cnt - step, nxt))
        else:
            pairs.append((lo, lo + step))

    def sort(lo, cnt):
        if cnt > 1:
            half = cnt // 2
            sort(lo, half)
            sort(lo + half, half)
            merge(lo, cnt, 1)

    sort(0, n)
    return pairs


SORT16 = _sort_pairs(PEER_TOPK)
BITONIC16 = _bitonic_merge_pairs(PEER_TOPK)


def _cparams(sem):
    return pltpu.CompilerParams(dimension_semantics=sem, vmem_limit_bytes=VMEM_LIMIT)


def _const_spec(shape):
    nd = len(shape)
    return pl.BlockSpec(shape, lambda *_: (0,) * nd, pipeline_mode=pl.Buffered(1))


def _layer_spec(stacked, layer):
    shape = stacked.shape[1:]
    return pl.BlockSpec((None,) + shape, lambda *_: (layer,) + (0,) * len(shape), pipeline_mode=pl.Buffered(1))


def _gelu(x):
    return 0.5 * x * (1.0 + jnp.tanh(0.7978845608028654 * (x + 0.044715 * (x * x * x))))


def _rms_rows(x, gain):
    return x * lax.rsqrt(jnp.mean(x * x, axis=-1, keepdims=True) + EPS) * gain


def _group_sumsq(x, ones_blockdiag):
    sq = x * x
    hi = sq.astype(BF16)
    lo = (sq - hi.astype(F32)).astype(BF16)
    return (jnp.dot(hi, ones_blockdiag, preferred_element_type=F32)
            + jnp.dot(lo, ones_blockdiag, preferred_element_type=F32))


def _swap_halves(x, group, lo_mask):
    w = x.shape[-1]
    half = group // 2
    return jnp.where(lo_mask, pltpu.roll(x, w - half, 1), pltpu.roll(x, half, 1))


def _ada_kernel(c_ref, w_ref, b_ref, o_ref):
    c = c_ref[...]
    ca = c * jax.nn.sigmoid(c)
    o_ref[0] = jnp.dot(ca, w_ref[0], preferred_element_type=F32,
                       precision=lax.Precision.HIGHEST) + b_ref[0]


def _ada_call(c_pad, w_ada, b_ada):
    depth, d, n = w_ada.shape
    tn = 1536
    return pl.pallas_call(
        _ada_kernel,
        grid=(depth, n // tn),
        in_specs=[
            pl.BlockSpec((8, d), lambda l, j: (0, 0)),
            pl.BlockSpec((1, d, tn), lambda l, j: (l, 0, j)),
            pl.BlockSpec((1, 1, tn), lambda l, j: (l, 0, j)),
        ],
        out_specs=pl.BlockSpec((1, 8, tn), lambda l, j: (l, 0, j)),
        out_shape=jax.ShapeDtypeStruct((depth, 8, n), F32),
        compiler_params=_cparams(("arbitrary", "arbitrary")),
        name="ada_mod",
    )(c_pad, w_ada, b_ada.reshape(depth, 1, n))


def _inproj_kernel(x_ref, mod_ref, n1g_ref, win_ref, avg_ref, ws_ref, bmat_ref, ones_ref,
                   qg_ref, kg_ref, cosb_ref, sinb_ref, cqg_ref, wuq_ref, ckvg_ref, wukv_ref,
                   cosc_ref, sinc_ref,
                   ya_ref, gates_ref, qb_ref, kb_ref, vtb_ref, qc_ref, kc_ref, vtc_ref):
    tm = x_ref.shape[0]
    d = D_MODEL
    shift = mod_ref[0, 0:1, :]
    scale = mod_ref[0, 1:2, :]
    h = (_rms_rows(x_ref[...], n1g_ref[...]) * (1.0 + scale) + shift).astype(BF16)

    za = jnp.dot(h, win_ref[:, OFF_A:OFF_A + COLS_A], preferred_element_type=F32)
    z = _gelu(za)
    u = z[:, :A_WIDTH]
    vn = _rms_rows(z[:, A_WIDTH:], avg_ref[...]).astype(BF16)
    grp = lax.broadcasted_iota(jnp.int32, (BLOCK, A_WIDTH), 1) // A_DIM
    for c in range(tm // BLOCK):
        vc = vn[c * BLOCK:(c + 1) * BLOCK]
        mixed = jnp.zeros((BLOCK, A_WIDTH), F32)
        for g in range(A_GROUPS):
            r = jnp.dot(ws_ref[g], vc, preferred_element_type=F32)
            mixed = jnp.where(grp == g, r, mixed)
        ya_ref[c * BLOCK:(c + 1) * BLOCK, :] = (
            u[c * BLOCK:(c + 1) * BLOCK] * (mixed + bmat_ref[...])).astype(BF16)

    zb = jnp.dot(h, win_ref[:, OFF_B:OFF_B + COLS_B], preferred_element_type=F32)
    qw = B_HEADS * B_HEAD_DIM
    kw = B_KV_HEADS * B_HEAD_DIM
    cosb = cosb_ref[...]
    sinb = sinb_ref[...]
    lane_q = lax.broadcasted_iota(jnp.int32, (tm, qw), 1)
    q = zb[:, :qw]
    qn = q * lax.rsqrt(_group_sumsq(q, ones_ref[...]) * (1.0 / B_HEAD_DIM) + EPS) * qg_ref[...]
    reps = qw // LANE
    qr = (qn * jnp.concatenate([cosb] * reps, axis=1)
          + _swap_halves(qn, B_HEAD_DIM, (lane_q % B_HEAD_DIM) < B_HEAD_DIM // 2)
          * jnp.concatenate([sinb] * reps, axis=1)) * (B_HEAD_DIM ** -0.5 * LOG2E)
    lane_t = lax.broadcasted_iota(jnp.int32, (tm, LANE), 1)
    for t in range(reps):
        tile = qr[:, t * LANE:(t + 1) * LANE]
        qb_ref[:, t * LANE:(t + 1) * LANE] = jnp.where(lane_t < B_HEAD_DIM, tile, 0.0).astype(BF16)
        qb_ref[:, (reps + t) * LANE:(reps + t + 1) * LANE] = (
            jnp.where(lane_t >= B_HEAD_DIM, tile, 0.0).astype(BF16))
    k = zb[:, qw:qw + kw]
    kn = k * lax.rsqrt(_group_sumsq(k, ones_ref[0:kw, 0:kw]) * (1.0 / B_HEAD_DIM) + EPS) * kg_ref[...]
    kr = kn * cosb + _swap_halves(kn, B_HEAD_DIM, (lane_t % B_HEAD_DIM) < B_HEAD_DIM // 2) * sinb
    kb_ref[...] = kr.astype(BF16)
    vtb_ref[0] = zb[:, qw + kw:].T.astype(BF16)

    zc = jnp.dot(h, win_ref[:, OFF_C:OFF_G], preferred_element_type=F32)
    cosc = cosc_ref[...]
    sinc = sinc_ref[...]
    rope_lo = C_NOPE + C_ROPE // 2
    cqn = _rms_rows(zc[:, :C_Q_RANK], cqg_ref[...]).astype(BF16)
    qf = jnp.dot(cqn, wuq_ref[...], preferred_element_type=F32)
    lane_c = lax.broadcasted_iota(jnp.int32, qf.shape, 1)
    qrot = (qf * jnp.concatenate([cosc] * C_HEADS, axis=1)
            + _swap_halves(qf, C_ROPE, (lane_c % LANE) < rope_lo)
            * jnp.concatenate([sinc] * C_HEADS, axis=1)) * ((C_NOPE + C_ROPE) ** -0.5 * LOG2E)
    qc_ref[...] = qrot.astype(BF16)
    ckvn = _rms_rows(zc[:, C_Q_RANK:C_Q_RANK + C_KV_RANK], ckvg_ref[...]).astype(BF16)
    kv = jnp.dot(ckvn, wukv_ref[...], preferred_element_type=F32)
    krope = zc[:, C_Q_RANK + C_KV_RANK:]
    krot = krope * cosc + _swap_halves(krope, C_ROPE, lane_t < rope_lo) * sinc
    kc_ref[...] = (kv[:, :C_HEADS * LANE] + jnp.concatenate([krot] * C_HEADS, axis=1)).astype(BF16)
    vtc_ref[0] = kv[:, C_HEADS * LANE:].T.astype(BF16)

    for n in range(N_BRANCH):
        zg = jnp.dot(h, win_ref[:, OFF_G + n * d:OFF_G + (n + 1) * d], preferred_element_type=F32)
        gates_ref[:, n * d:(n + 1) * d] = jax.nn.sigmoid(zg).astype(BF16)


def _inproj_call(x, mod, lw, tabs, layer, tm, seq):
    t, d = x.shape
    nb = t // tm
    per_seq = seq // tm
    row = lambda i: (i, 0)
    pos = lambda i: (i % per_seq, 0)
    per_layer = lambda name: _layer_spec(lw[name], layer)
    in_specs = [
        pl.BlockSpec((tm, d), row),
        pl.BlockSpec((None, 1, 6, d), lambda i: (layer, i // per_seq, 0, 0)),
        per_layer("n1g"),
        per_layer("w_in"),
        per_layer("a_v_gain"),
        per_layer("w_s"),
        per_layer("bmat"),
        _const_spec((B_HEADS * B_HEAD_DIM, B_HEADS * B_HEAD_DIM)),
        per_layer("qg"),
        per_layer("kg"),
        pl.BlockSpec((tm, LANE), pos),
        pl.BlockSpec((tm, LANE), pos),
        per_layer("cqg"),
        per_layer("w_uq"),
        per_layer("ckvg"),
        per_layer("w_ukv"),
        pl.BlockSpec((tm, LANE), pos),
        pl.BlockSpec((tm, LANE), pos),
    ]
    out_shapes = (
        jax.ShapeDtypeStruct((t, A_WIDTH), BF16),
        jax.ShapeDtypeStruct((t, N_BRANCH * d), BF16),
        jax.ShapeDtypeStruct((t, B_HEADS * LANE), BF16),
        jax.ShapeDtypeStruct((t, LANE), BF16),
        jax.ShapeDtypeStruct((nb, B_KV_HEADS * B_HEAD_DIM, tm), BF16),
        jax.ShapeDtypeStruct((t, C_HEADS * LANE), BF16),
        jax.ShapeDtypeStruct((t, C_HEADS * LANE), BF16),
        jax.ShapeDtypeStruct((nb, C_HEADS * C_V, tm), BF16),
    )
    out_specs = (
        pl.BlockSpec((tm, A_WIDTH), row),
        pl.BlockSpec((tm, N_BRANCH * d), row),
        pl.BlockSpec((tm, B_HEADS * LANE), row),
        pl.BlockSpec((tm, LANE), row),
        pl.BlockSpec((1, B_KV_HEADS * B_HEAD_DIM, tm), lambda i: (i, 0, 0)),
        pl.BlockSpec((tm, C_HEADS * LANE), row),
        pl.BlockSpec((tm, C_HEADS * LANE), row),
        pl.BlockSpec((1, C_HEADS * C_V, tm), lambda i: (i, 0, 0)),
    )
    return pl.pallas_call(
        _inproj_kernel,
        grid=(nb,),
        in_specs=in_specs,
        out_specs=out_specs,
        out_shape=out_shapes,
        compiler_params=_cparams(("arbitrary",)),
        name="in_proj",
    )(x, mod, lw["n1g"], lw["w_in"], lw["a_v_gain"], lw["w_s"], lw["bmat"], tabs["ones"],
      lw["qg"], lw["kg"], tabs["cosb"], tabs["sinb"], lw["cqg"], lw["w_uq"], lw["ckvg"], lw["w_ukv"],
      tabs["cosc"], tabs["sinc"])


def _attn_kernel(q_ref, k_ref, vt_ref, o_ref, s_ref, *, heads_share_kv, dv):
    tq = q_ref.shape[0]
    n_slabs, _, slab = vt_ref.shape
    tk = s_ref.shape[2]
    group = tk // slab
    n_chunks = n_slabs // group
    ones = jnp.ones((16, tk), BF16)
    hs = range(2)
    qs = [q_ref[:, hh * LANE:(hh + 1) * LANE] for hh in hs]

    def scores(c, slot):
        start = pl.multiple_of(c * tk, tk)
        s_t = [lax.dot_general(k_ref[pl.ds(start, tk), (0 if heads_share_kv else hh * LANE):
                                     (LANE if heads_share_kv else (hh + 1) * LANE)],
                               qs[hh], (((1,), (1,)), ((), ())), preferred_element_type=F32) for hh in hs]
        for hh in hs:
            s_ref[slot, hh] = s_t[hh]
        return tuple(jnp.max(s_t[hh], axis=0, keepdims=True) for hh in hs)

    def accumulate(c, slot, state, cmax):
        m_new = [jnp.maximum(state[hh][0], cmax[hh]) for hh in hs]
        p = [jnp.exp2(s_ref[slot, hh] - m_new[hh]).astype(BF16) for hh in hs]
        alpha = [jnp.exp2(state[hh][0] - m_new[hh]) for hh in hs]
        vt = []
        for hh in hs:
            v_rows = slice(0, dv) if heads_share_kv else slice(hh * dv, (hh + 1) * dv)
            v_t = jnp.concatenate([vt_ref[c * group + g, v_rows, :] for g in range(group)], axis=1)
            vt.append(jnp.concatenate([v_t, ones], axis=0))
        pv = [jnp.dot(vt[hh], p[hh], preferred_element_type=F32) for hh in hs]
        return tuple((m_new[hh], state[hh][1] * alpha[hh] + pv[hh]) for hh in hs)

    def body(i, carry):
        state, cmax = carry
        c = 2 * i
        cmax1 = scores(c + 1, 1)
        state = accumulate(c, 0, state, cmax)
        cmax0 = scores(jnp.minimum(c + 2, n_chunks - 1), 0)
        state = accumulate(c + 1, 1, state, cmax1)
        return state, cmax0

    state = tuple((jnp.full((1, tq), -jnp.inf, F32), jnp.zeros((dv + 16, tq), F32)) for _ in hs)
    cmax = scores(0, 0)
    if n_chunks > 1:
        assert n_chunks % 2 == 0
        state, _ = lax.fori_loop(0, n_chunks // 2, body, (state, cmax))
    else:
        state = accumulate(0, 0, state, cmax)
    outs = [acc[:dv] * (1.0 / acc[dv:dv + 1]) for _, acc in state]
    o_ref[...] = jnp.concatenate(outs, axis=0).T.astype(BF16)


def _attn_call(q, k, vt, batch, seq, tq, heads_share_kv, q_heads_per_kv):
    t = q.shape[0]
    n_heads = q.shape[1] // LANE
    tk = vt.shape[2]
    dv = 64
    n_chunks = seq // tk
    nq = seq // tq
    group = 2 if n_chunks % 4 == 0 else 1
    if heads_share_kv:
        k_spec = pl.BlockSpec((seq, LANE), lambda b, j, i: (b, 0))
        pairs_per_kv = q_heads_per_kv // 2
        vt_spec = pl.BlockSpec((n_chunks, dv, tk), lambda b, j, i: (b, j // pairs_per_kv, 0))
    else:
        k_spec = pl.BlockSpec((seq, 2 * LANE), lambda b, j, i: (b, j))
        vt_spec = pl.BlockSpec((n_chunks, 2 * dv, tk), lambda b, j, i: (b, j, 0))
    return pl.pallas_call(
        functools.partial(_attn_kernel, heads_share_kv=heads_share_kv, dv=dv),
        grid=(batch, n_heads // 2, nq),
        in_specs=[pl.BlockSpec((tq, 2 * LANE), lambda b, j, i: (b * nq + i, j)), k_spec, vt_spec],
        out_specs=pl.BlockSpec((tq, 2 * dv), lambda b, j, i: (b * nq + i, j)),
        out_shape=jax.ShapeDtypeStruct((t, n_heads * dv), BF16),
        compiler_params=_cparams(("arbitrary", "arbitrary", "arbitrary")),
        scratch_shapes=[pltpu.VMEM((2, 2, group * tk, tq), F32)],
        name="attn_shared_kv" if heads_share_kv else "attn_latent",
    )(q, k, vt)


def _merge_kernel(ya_ref, yb_ref, yc_ref, gates_ref, x_ref, mod_ref, n2g_ref, wpa_ref, wpb_ref,
                  wpc_ref, wout_ref, wq_ref, sk_ref,
                  x1_ref, h2_ref, lrow_ref, e1_ref, r2_ref, e2_ref, vals_ref, s_scr):
    d = D_MODEL
    merged = (gates_ref[:, 0:d].astype(F32) * jnp.dot(ya_ref[...], wpa_ref[...], preferred_element_type=F32)
              + gates_ref[:, d:2 * d].astype(F32) * jnp.dot(yb_ref[...], wpb_ref[...], preferred_element_type=F32)
              + gates_ref[:, 2 * d:3 * d].astype(F32) * jnp.dot(yc_ref[...], wpc_ref[...], preferred_element_type=F32))
    x1 = x_ref[...] + mod_ref[0, 2:3, :] * jnp.dot(merged.astype(BF16), wout_ref[...],
                                                   preferred_element_type=F32)
    x1_ref[...] = x1
    h2 = (_rms_rows(x1, n2g_ref[...]) * (1.0 + mod_ref[0, 4:5, :]) + mod_ref[0, 3:4, :]).astype(BF16)
    h2_ref[...] = h2
    qk = jnp.dot(h2, wq_ref[...], preferred_element_type=F32).astype(BF16)

    sub = 8
    n_tiles = PEER_N_KEYS // sub
    assert n_tiles == PEER_TOPK
    for h in range(PEER_HEADS):
        for p in range(2):
            idx = 2 * h + p
            s = lax.dot_general(sk_ref[idx], qk[:, idx * PEER_HALF:(idx + 1) * PEER_HALF],
                                (((1,), (1,)), ((), ())), preferred_element_type=F32)
            s_scr[p, h] = s
            for t0 in range(0, s.shape[1], LANE):
                tiles = [s[k * sub:(k + 1) * sub, t0:t0 + LANE] for k in range(n_tiles)]
                top = list(tiles)
                for i, j in SORT16:
                    top[i], top[j] = jnp.maximum(top[i], top[j]), jnp.minimum(top[i], top[j])
                for shift in (4, 2, 1):
                    other = [pltpu.roll(v, shift, 0) for v in top]
                    top = [jnp.maximum(top[k], other[n_tiles - 1 - k]) for k in range(n_tiles)]
                    for i, j in BITONIC16:
                        top[i], top[j] = jnp.maximum(top[i], top[j]), jnp.minimum(top[i], top[j])
                for a in range(PEER_TOPK):
                    vals_ref[p, a, h:h + 1, t0:t0 + LANE] = top[a][0:1, :]
                if p == 1:
                    for k in range(0, n_tiles, 2):
                        rank = [functools.reduce(jnp.add, [jnp.where(tiles[k + i] < top[a], 1.0, 0.0)
                                                           for a in range(PEER_TOPK)]) for i in range(2)]
                        r2_ref[h, k * sub:(k + 2) * sub, t0:t0 + LANE] = jnp.concatenate(rank, axis=0).astype(BF16)

    v1 = [vals_ref[0, a] for a in range(PEER_TOPK)]
    v2 = [vals_ref[1, a] for a in range(PEER_TOPK)]
    cands = [v1[a] + v2[b] for a, b in CAND_PAIRS]
    work = list(cands)
    tau = None
    for it in range(PEER_TOPK):
        mx = functools.reduce(jnp.maximum, work)
        if it == PEER_TOPK - 1:
            tau = mx
        else:
            work = [jnp.where(w == mx, -jnp.inf, w) for w in work]
    top = v1[0] + v2[0]
    zsum = functools.reduce(jnp.add, [jnp.where(c >= tau, jnp.exp(c - top), 0.0) for c in cands])
    rz = 1.0 / zsum
    count = [functools.reduce(jnp.add, [jnp.where(c >= tau, 1.0, 0.0)
                                         for c, (a2, _) in zip(cands, CAND_PAIRS) if a2 == a])
             for a in range(PEER_TOPK)]
    for h in range(PEER_HEADS):
        s1 = s_scr[0, h]
        lrow = jnp.zeros(s1.shape, F32)
        for a in range(PEER_TOPK):
            lrow = jnp.where(s1 == v1[a][h:h + 1, :], count[a][h:h + 1, :], lrow)
        lrow_ref[h] = lrow
        e1_ref[h] = jnp.exp(s1 - v1[0][h:h + 1, :])
        e2_ref[h] = (jnp.exp(s_scr[1, h] - v2[0][h:h + 1, :]) * rz[h:h + 1, :]).astype(BF16)


def _merge_call(ya, yb, yc, gates, x, mod, lw, layer, tm, seq):
    t, d = x.shape
    nb = t // tm
    per_seq = seq // tm
    row = lambda i: (i, 0)
    per_layer = lambda name: _layer_spec(lw[name], layer)
    hk = pl.BlockSpec((PEER_HEADS, PEER_N_KEYS, tm), lambda i: (0, 0, i))
    hk_f32 = jax.ShapeDtypeStruct((PEER_HEADS, PEER_N_KEYS, t), F32)
    hk_bf16 = jax.ShapeDtypeStruct((PEER_HEADS, PEER_N_KEYS, t), BF16)
    return pl.pallas_call(
        _merge_kernel,
        grid=(nb,),
        in_specs=[
            pl.BlockSpec((tm, A_WIDTH), row),
            pl.BlockSpec((tm, A_WIDTH), row),
            pl.BlockSpec((tm, A_WIDTH), row),
            pl.BlockSpec((tm, N_BRANCH * d), row),
            pl.BlockSpec((tm, d), row),
            pl.BlockSpec((None, 1, 6, d), lambda i: (layer, i // per_seq, 0, 0)),
            per_layer("n2g"),
            per_layer("w_pa"),
            per_layer("w_pb"),
            per_layer("w_pc"),
            per_layer("w_out"),
            per_layer("w_query"),
            per_layer("sub_keys"),
        ],
        out_specs=(
            pl.BlockSpec((tm, d), row),
            pl.BlockSpec((tm, d), row),
            hk, hk, hk, hk,
        ),
        out_shape=(
            jax.ShapeDtypeStruct((t, d), F32),
            jax.ShapeDtypeStruct((t, d), BF16),
            hk_f32, hk_f32, hk_bf16, hk_bf16,
        ),
        scratch_shapes=[pltpu.VMEM((2, PEER_TOPK, PEER_HEADS, tm), F32),
                        pltpu.VMEM((2, PEER_HEADS, PEER_N_KEYS, tm), F32)],
        compiler_params=_cparams(("arbitrary",)),
        name="merge_route",
    )(ya, yb, yc, gates, x, mod, lw["n2g"], lw["w_pa"], lw["w_pb"], lw["w_pc"], lw["w_out"],
      lw["w_query"], lw["sub_keys"])


def _expert_kernel(h2_ref, lrow_ref, e1_ref, r2_ref, e2_ref, u_ref, v_ref, x1_ref, mod_ref,
                   fg_ref, o_ref, acc_ref, w_ref, *, final_norm):
    ch = pl.program_id(1)
    rows = lrow_ref.shape[1]
    tm = h2_ref.shape[0]
    pack_rows = 16

    @pl.when(ch == 0)
    def _():
        acc_ref[...] = jnp.zeros_like(acc_ref)

    def row_tile(ref, h, r):
        row = jnp.broadcast_to(ref[h, r:r + 1, :], (pack_rows, tm)).astype(BF16)
        return jnp.concatenate([row] * (PEER_N_KEYS // pack_rows), axis=0)

    zero = jnp.zeros((PEER_N_KEYS, tm), BF16)
    for r in range(rows):
        gate = None
        for h in range(PEER_HEADS):
            sel = jnp.where(r2_ref[h] < row_tile(lrow_ref, h, r), e2_ref[h], zero) * row_tile(e1_ref, h, r)
            gate = sel if gate is None else gate + sel
        rs = slice(r * PEER_N_KEYS, (r + 1) * PEER_N_KEYS)
        w_ref[rs, :] = gate
    act = lax.dot_general(u_ref[...], h2_ref[...], (((1,), (1,)), ((), ())), preferred_element_type=F32)
    for r in range(rows):
        rs = slice(r * PEER_N_KEYS, (r + 1) * PEER_N_KEYS)
        w_ref[rs, :] = w_ref[rs, :] * _gelu(act[rs, :].astype(BF16))
    acc_ref[...] += lax.dot_general(v_ref[...], w_ref[...], (((0,), (0,)), ((), ())),
                                    preferred_element_type=F32)

    @pl.when(ch == pl.num_programs(1) - 1)
    def _():
        x2 = x1_ref[...] + mod_ref[0, 5:6, :] * acc_ref[...].T
        if final_norm:
            x2 = _rms_rows(x2, fg_ref[...])
        o_ref[...] = x2


def _expert_call(h2, lrow, e1, r2, e2, u, v, x1, mod, final_gain, layer, tm, ec, seq, final_norm):
    t, d = x1.shape
    n_exp = u.shape[1]
    rows = ec // PEER_N_KEYS
    per_seq = seq // tm
    tok = lambda i, c: (i, 0)
    chunk_rows = pl.BlockSpec((PEER_HEADS, rows, tm), lambda i, c: (0, c, i))
    all_rows = pl.BlockSpec((PEER_HEADS, PEER_N_KEYS, tm), lambda i, c: (0, 0, i))
    return pl.pallas_call(
        functools.partial(_expert_kernel, final_norm=final_norm),
        grid=(t // tm, n_exp // ec),
        in_specs=[
            pl.BlockSpec((tm, d), tok),
            chunk_rows, chunk_rows, all_rows, all_rows,
            pl.BlockSpec((None, ec, d), lambda i, c: (layer, c, 0)),
            pl.BlockSpec((None, ec, d), lambda i, c: (layer, c, 0)),
            pl.BlockSpec((tm, d), tok),
            pl.BlockSpec((None, 1, 6, d), lambda i, c: (layer, i // per_seq, 0, 0)),
            pl.BlockSpec((1, d), lambda i, c: (0, 0)),
        ],
        out_specs=pl.BlockSpec((tm, d), tok),
        out_shape=jax.ShapeDtypeStruct((t, d), F32),
        scratch_shapes=[pltpu.VMEM((d, tm), F32), pltpu.VMEM((ec, tm), BF16)],
        compiler_params=_cparams(("arbitrary", "arbitrary")),
        name="peer_dense",
    )(h2, lrow, e1, r2, e2, u, v, x1, mod, final_gain)


def _w_in_column_map():
    src = np.full((W_IN_COLS,), -1, np.int64)
    src[OFF_A:OFF_A + COLS_A] = np.arange(COLS_A)
    pair = np.concatenate([np.arange(0, B_HEAD_DIM, 2), np.arange(1, B_HEAD_DIM, 2)])
    tiles = B_HEADS * B_HEAD_DIM // LANE
    for t in range(tiles):
        for side, head in enumerate((t, tiles + t)):
            dst = OFF_B + t * LANE + side * B_HEAD_DIM
            src[dst:dst + B_HEAD_DIM] = COLS_A + head * B_HEAD_DIM + pair
    qw = B_HEADS * B_HEAD_DIM
    for kv in range(B_KV_HEADS):
        dst = OFF_B + qw + kv * B_HEAD_DIM
        src[dst:dst + B_HEAD_DIM] = COLS_A + qw + kv * B_HEAD_DIM + pair
    kw = B_KV_HEADS * B_HEAD_DIM
    src[OFF_B + qw + kw:OFF_B + qw + 2 * kw] = COLS_A + qw + kw + np.arange(kw)
    base_c = COLS_A + COLS_B
    src[OFF_C:OFF_C + C_Q_RANK + C_KV_RANK] = base_c + np.arange(C_Q_RANK + C_KV_RANK)
    rope_src = base_c + C_Q_RANK + C_KV_RANK
    dst = OFF_C + C_Q_RANK + C_KV_RANK + C_NOPE
    src[dst:dst + C_ROPE // 2] = rope_src + np.arange(0, C_ROPE, 2)
    src[dst + C_ROPE // 2:dst + C_ROPE] = rope_src + np.arange(1, C_ROPE, 2)
    src[OFF_G:] = base_c + COLS_C + np.arange(N_BRANCH * D_MODEL)
    return src


def _take_cols(w, src):
    parts, i, n = [], 0, len(src)
    while i < n:
        j = i + 1
        if src[i] < 0:
            while j < n and src[j] < 0:
                j += 1
            parts.append(jnp.zeros(w.shape[:-1] + (j - i,), w.dtype))
        else:
            step = int(src[j] - src[i]) if j < n and src[j] > src[i] else 1
            while j < n and src[j] == src[i] + (j - i) * step:
                j += 1
            parts.append(lax.slice_in_dim(w, int(src[i]), int(src[j - 1]) + 1, stride=step, axis=w.ndim - 1))
        i = j
    return jnp.concatenate(parts, axis=-1)


def _uq_column_map():
    src = np.full((C_HEADS * LANE,), -1, np.int64)
    per = C_NOPE + C_ROPE
    for h in range(C_HEADS):
        src[h * LANE:h * LANE + C_NOPE] = h * per + np.arange(C_NOPE)
        src[h * LANE + C_NOPE:h * LANE + C_NOPE + C_ROPE // 2] = h * per + C_NOPE + np.arange(0, C_ROPE, 2)
        src[h * LANE + C_NOPE + C_ROPE // 2:h * LANE + per] = h * per + C_NOPE + np.arange(1, C_ROPE, 2)
    return src


def _ukv_column_map():
    src = np.full((C_HEADS * (LANE + C_V),), -1, np.int64)
    per = C_NOPE + C_V
    for h in range(C_HEADS):
        src[h * LANE:h * LANE + C_NOPE] = h * per + np.arange(C_NOPE)
        src[C_HEADS * LANE + h * C_V:C_HEADS * LANE + (h + 1) * C_V] = h * per + C_NOPE + np.arange(C_V)
    return src


def _rope_tables(seq):
    rows = seq // GRID_W
    row = jnp.repeat(jnp.arange(rows, dtype=F32), GRID_W)
    col = jnp.tile(jnp.arange(GRID_W, dtype=F32), rows)

    def cos_sin(d_rot):
        half = d_rot // 2
        freq = ROPE_THETA ** (-jnp.arange(0, half, 2, dtype=F32) / half)
        ang = jnp.concatenate([row[:, None] * freq, col[:, None] * freq], axis=-1)
        return jnp.cos(ang), jnp.sin(ang)

    cb, sb = cos_sin(B_HEAD_DIM)
    cosb = jnp.tile(jnp.concatenate([cb, cb], axis=1), (1, LANE // B_HEAD_DIM))
    sinb = jnp.tile(jnp.concatenate([-sb, sb], axis=1), (1, LANE // B_HEAD_DIM))
    cc, sc = cos_sin(C_ROPE)
    one = jnp.ones((seq, C_NOPE), F32)
    pad = LANE - C_NOPE - C_ROPE
    cosc = jnp.concatenate([one, cc, cc, jnp.ones((seq, pad), F32)], axis=1)
    sinc = jnp.concatenate([0.0 * one, -sc, sc, jnp.zeros((seq, pad), F32)], axis=1)
    return cosb, sinb, cosc, sinc


def kernel(x, c, w_ada, b_ada, norm1_gain, w_in, a_v_gain, a_w_s, a_b_s, b_q_gain, b_k_gain,
           c_q_gain, c_w_uq, c_kv_gain, c_w_ukv, w_pa, w_pb, w_pc, w_out, norm2_gain,
           peer_w_query, peer_sub_keys, peer_u, peer_v, final_gain):
    batch, seq, d = x.shape
    depth = w_ada.shape[0]
    t = batch * seq
    assert d == D_MODEL and seq % GRID_W == 0
    tm = min(512, seq)
    tq = min(1024, seq)
    tm_merge = min(256, seq)
    tm_peer = min(1024, seq)
    ec = 1024
    assert seq % tm == 0 and seq % tq == 0

    cosb, sinb, cosc, sinc = _rope_tables(seq)
    qw = B_HEADS * B_HEAD_DIM
    ones = jnp.asarray(np.kron(np.eye(qw // B_HEAD_DIM), np.ones((B_HEAD_DIM, B_HEAD_DIM))), BF16)
    tabs = dict(cosb=cosb, sinb=sinb, cosc=cosc, sinc=sinc, ones=ones)

    c_pad = jnp.zeros((8, d), F32).at[:batch].set(c)
    mod_all = _ada_call(c_pad, w_ada, b_ada)[:, :batch, :].reshape(depth, batch, 6, d)

    pair = np.concatenate([np.arange(0, B_HEAD_DIM, 2), np.arange(1, B_HEAD_DIM, 2)])
    lw = dict(
        n1g=norm1_gain[:, None, :],
        w_in=_take_cols(w_in, _w_in_column_map()).astype(BF16),
        a_v_gain=a_v_gain[:, None, :],
        w_s=a_w_s.astype(BF16),
        bmat=jnp.repeat(jnp.swapaxes(a_b_s, 1, 2), A_DIM, axis=2),
        qg=jnp.tile(b_q_gain[:, pair], (1, B_HEADS))[:, None, :],
        kg=jnp.tile(b_k_gain[:, pair], (1, B_KV_HEADS))[:, None, :],
        cqg=c_q_gain[:, None, :],
        w_uq=_take_cols(c_w_uq, _uq_column_map()).astype(BF16),
        ckvg=c_kv_gain[:, None, :],
        w_ukv=_take_cols(c_w_ukv, _ukv_column_map()).astype(BF16),
        n2g=norm2_gain[:, None, :],
        w_pa=w_pa.astype(BF16),
        w_pb=w_pb.astype(BF16),
        w_pc=w_pc.astype(BF16),
        w_out=w_out.astype(BF16),
        w_query=peer_w_query.astype(BF16),
        sub_keys=peer_sub_keys.reshape(depth, 2 * PEER_HEADS, PEER_N_KEYS, PEER_HALF).astype(BF16),
    )
    u_b = peer_u.astype(BF16)
    v_b = peer_v.astype(BF16)

    xf = x.reshape(t, d)
    for l in range(depth):
        ya, gates, qb, kb, vtb, qc, kc, vtc = _inproj_call(xf, mod_all, lw, tabs, l, tm, seq)
        yb = _attn_call(qb, kb, vtb, batch, seq, tq, True, B_HEADS // B_KV_HEADS)
        yc = _attn_call(qc, kc, vtc, batch, seq, tq, False, 1)
        x1, h2, lrow, e1, r2, e2 = _merge_call(ya, yb, yc, gates, xf, mod_all, lw, l, tm_merge, seq)
        xf = _expert_call(h2, lrow, e1, r2, e2, u_b, v_b, x1, mod_all, final_gain[None, :],
                          l, tm_peer, ec, seq, l == depth - 1)
    return xf.reshape(batch, seq, d)
```

```python
import functools

import numpy as np
import jax
import jax.numpy as jnp
from jax import lax
from jax.experimental import pallas as pl
from jax.experimental.pallas import tpu as pltpu

F32 = jnp.float32
BF16 = jnp.bfloat16

LANE = 128
SUBLANES = 8
PACKED_ROWS = 16
VMEM_LIMIT = 56 * 1024 * 1024

D_MODEL = 1024
GRID_W = 64
BLOCK = 128
EPS = 1e-6
ROPE_THETA = 10000.0
LOG2E = 1.4426950408889634

A_GROUPS = 8
A_DIM = 64
A_WIDTH = A_GROUPS * A_DIM
B_HEADS = 8
B_KV_HEADS = 2
B_HEAD_DIM = 64
C_HEADS = 8
C_NOPE = 64
C_ROPE = 32
C_V = 64
C_Q_RANK = 256
C_KV_RANK = 128
COLS_A = 2 * A_WIDTH
COLS_B = (B_HEADS + 2 * B_KV_HEADS) * B_HEAD_DIM
COLS_C = C_Q_RANK + C_KV_RANK + C_ROPE
N_BRANCH = 3
PEER_HEADS = 8
PEER_N_KEYS = 128
PEER_HALF = 128
PEER_TOPK = 16

OFF_A = 0
OFF_B = OFF_A + COLS_A
OFF_C = OFF_B + COLS_B
OFF_G = OFF_C + C_Q_RANK + C_KV_RANK + LANE
W_IN_COLS = OFF_G + N_BRANCH * D_MODEL

CAND_PAIRS = [(a, b) for a in range(PEER_TOPK) for b in range(PEER_TOPK) if (a + 1) * (b + 1) <= PEER_TOPK]


def _bitonic_merge_pairs(n):
    pairs, stride = [], n // 2
    while stride:
        pairs += [(i, i + stride) for i in range(n) if not i & stride]
        stride //= 2
    return pairs


def _sort_pairs(n):
    pairs = []

    def merge(lo, cnt, step):
        nxt = step * 2
        if nxt < cnt:
            merge(lo, cnt, nxt)
            merge(lo + step, cnt, nxt)
            pairs.extend((i, i + step) for i in range(lo + step, lo + cnt - step, nxt))
        else:
            pairs.append((lo, lo + step))

    def sort(lo, cnt):
        if cnt > 1:
            half = cnt // 2
            sort(lo, half)
            sort(lo + half, half)
            merge(lo, cnt, 1)

    sort(0, n)
    return pairs


SORT16 = _sort_pairs(PEER_TOPK)
BITONIC16 = _bitonic_merge_pairs(PEER_TOPK)


def _tile_sizes(seq):
    tm = min(512, seq)
    tq = min(1024, seq)
    tm_merge = min(256, seq)
    tm_peer = min(1024, seq)
    ec = 8 * PEER_N_KEYS
    assert seq % tm == 0 and seq % tq == 0 and seq % tm_peer == 0 and seq % tm_merge == 0
    return tm, tq, tm_merge, tm_peer, ec


def _cparams(sem):
    return pltpu.CompilerParams(dimension_semantics=sem, vmem_limit_bytes=VMEM_LIMIT)


def _const_spec(shape):
    nd = len(shape)
    return pl.BlockSpec(shape, lambda *_: (0,) * nd, pipeline_mode=pl.Buffered(1))


def _layer_spec(stacked, layer):
    shape = stacked.shape[1:]
    return pl.BlockSpec((None,) + shape, lambda *_: (layer,) + (0,) * len(shape), pipeline_mode=pl.Buffered(1))


def _gelu(x):
    return 0.5 * x * (1.0 + jnp.tanh(0.7978845608028654 * (x + 0.044715 * (x * x * x))))


def _rms_rows(x, gain):
    return x * lax.rsqrt(jnp.mean(x * x, axis=-1, keepdims=True) + EPS) * gain


def _group_sumsq(x, ones_blockdiag):
    sq = x * x
    hi = sq.astype(BF16)
    lo = (sq - hi.astype(F32)).astype(BF16)
    return (jnp.dot(hi, ones_blockdiag, preferred_element_type=F32)
            + jnp.dot(lo, ones_blockdiag, preferred_element_type=F32))


def _swap_halves(x, group, lo_mask):
    w = x.shape[-1]
    half = group // 2
    return jnp.where(lo_mask, pltpu.roll(x, w - half, 1), pltpu.roll(x, half, 1))


def _ada_kernel(c_ref, w_ref, b_ref, o_ref):
    c = c_ref[...]
    ca = c * jax.nn.sigmoid(c)
    o_ref[0] = jnp.dot(ca, w_ref[0], preferred_element_type=F32,
                       precision=lax.Precision.HIGHEST) + b_ref[0]


def _ada_call(c_pad, w_ada, b_ada):
    depth, d, n = w_ada.shape
    tn = n // 4
    return pl.pallas_call(
        _ada_kernel,
        grid=(depth, n // tn),
        in_specs=[
            pl.BlockSpec((SUBLANES, d), lambda l, j: (0, 0)),
            pl.BlockSpec((1, d, tn), lambda l, j: (l, 0, j)),
            pl.BlockSpec((1, 1, tn), lambda l, j: (l, 0, j)),
        ],
        out_specs=pl.BlockSpec((1, SUBLANES, tn), lambda l, j: (l, 0, j)),
        out_shape=jax.ShapeDtypeStruct((depth, SUBLANES, n), F32),
        compiler_params=_cparams(("arbitrary", "arbitrary")),
        name="ada_mod",
    )(c_pad, w_ada, b_ada.reshape(depth, 1, n))


def _inproj_kernel(x_ref, mod_ref, n1g_ref, win_ref, avg_ref, ws_ref, bmat_ref, ones_ref,
                   qg_ref, kg_ref, cosb_ref, sinb_ref, cqg_ref, wuq_ref, ckvg_ref, wukv_ref,
                   cosc_ref, sinc_ref,
                   ya_ref, gates_ref, qb_ref, kb_ref, vtb_ref, qc_ref, kc_ref, vtc_ref):
    tm = x_ref.shape[0]
    d = D_MODEL
    shift = mod_ref[0, 0:1, :]
    scale = mod_ref[0, 1:2, :]
    h = (_rms_rows(x_ref[...], n1g_ref[...]) * (1.0 + scale) + shift).astype(BF16)

    za = jnp.dot(h, win_ref[:, OFF_A:OFF_A + COLS_A], preferred_element_type=F32)
    z = _gelu(za)
    u = z[:, :A_WIDTH]
    vn = _rms_rows(z[:, A_WIDTH:], avg_ref[...]).astype(BF16)
    grp = lax.broadcasted_iota(jnp.int32, (BLOCK, A_WIDTH), 1) // A_DIM
    for c in range(tm // BLOCK):
        vc = vn[c * BLOCK:(c + 1) * BLOCK]
        mixed = jnp.zeros((BLOCK, A_WIDTH), F32)
        for g in range(A_GROUPS):
            r = jnp.dot(ws_ref[g], vc, preferred_element_type=F32)
            mixed = jnp.where(grp == g, r, mixed)
        ya_ref[c * BLOCK:(c + 1) * BLOCK, :] = (
            u[c * BLOCK:(c + 1) * BLOCK] * (mixed + bmat_ref[...])).astype(BF16)

    zb = jnp.dot(h, win_ref[:, OFF_B:OFF_B + COLS_B], preferred_element_type=F32)
    qw = B_HEADS * B_HEAD_DIM
    kw = B_KV_HEADS * B_HEAD_DIM
    cosb = cosb_ref[...]
    sinb = sinb_ref[...]
    lane_q = lax.broadcasted_iota(jnp.int32, (tm, qw), 1)
    q = zb[:, :qw]
    qn = q * lax.rsqrt(_group_sumsq(q, ones_ref[...]) * (1.0 / B_HEAD_DIM) + EPS) * qg_ref[...]
    reps = qw // LANE
    qr = (qn * jnp.concatenate([cosb] * reps, axis=1)
          + _swap_halves(qn, B_HEAD_DIM, (lane_q % B_HEAD_DIM) < B_HEAD_DIM // 2)
          * jnp.concatenate([sinb] * reps, axis=1)) * (B_HEAD_DIM ** -0.5 * LOG2E)
    lane_t = lax.broadcasted_iota(jnp.int32, (tm, LANE), 1)
    for t in range(reps):
        tile = qr[:, t * LANE:(t + 1) * LANE]
        qb_ref[:, t * LANE:(t + 1) * LANE] = jnp.where(lane_t < B_HEAD_DIM, tile, 0.0).astype(BF16)
        qb_ref[:, (reps + t) * LANE:(reps + t + 1) * LANE] = (
            jnp.where(lane_t >= B_HEAD_DIM, tile, 0.0).astype(BF16))
    k = zb[:, qw:qw + kw]
    kn = k * lax.rsqrt(_group_sumsq(k, ones_ref[0:kw, 0:kw]) * (1.0 / B_HEAD_DIM) + EPS) * kg_ref[...]
    kr = kn * cosb + _swap_halves(kn, B_HEAD_DIM, (lane_t % B_HEAD_DIM) < B_HEAD_DIM // 2) * sinb
    kb_ref[...] = kr.astype(BF16)
    vtb_ref[0] = zb[:, qw + kw:].T.astype(BF16)

    zc = jnp.dot(h, win_ref[:, OFF_C:OFF_G], preferred_element_type=F32)
    cosc = cosc_ref[...]
    sinc = sinc_ref[...]
    rope_lo = C_NOPE + C_ROPE // 2
    cqn = _rms_rows(zc[:, :C_Q_RANK], cqg_ref[...]).astype(BF16)
    qf = jnp.dot(cqn, wuq_ref[...], preferred_element_type=F32)
    lane_c = lax.broadcasted_iota(jnp.int32, qf.shape, 1)
    qrot = (qf * jnp.concatenate([cosc] * C_HEADS, axis=1)
            + _swap_halves(qf, C_ROPE, (lane_c % LANE) < rope_lo)
            * jnp.concatenate([sinc] * C_HEADS, axis=1)) * ((C_NOPE + C_ROPE) ** -0.5 * LOG2E)
    qc_ref[...] = qrot.astype(BF16)
    ckvn = _rms_rows(zc[:, C_Q_RANK:C_Q_RANK + C_KV_RANK], ckvg_ref[...]).astype(BF16)
    kv = jnp.dot(ckvn, wukv_ref[...], preferred_element_type=F32)
    krope = zc[:, C_Q_RANK + C_KV_RANK:]
    krot = krope * cosc + _swap_halves(krope, C_ROPE, lane_t < rope_lo) * sinc
    kc_ref[...] = (kv[:, :C_HEADS * LANE] + jnp.concatenate([krot] * C_HEADS, axis=1)).astype(BF16)
    vtc_ref[0] = kv[:, C_HEADS * LANE:].T.astype(BF16)

    for n in range(N_BRANCH):
        zg = jnp.dot(h, win_ref[:, OFF_G + n * d:OFF_G + (n + 1) * d], preferred_element_type=F32)
        gates_ref[:, n * d:(n + 1) * d] = jax.nn.sigmoid(zg).astype(BF16)


def _inproj_call(x, mod, lw, tabs, layer, tm, seq):
    t, d = x.shape
    nb = t // tm
    per_seq = seq // tm
    row = lambda i: (i, 0)
    pos = lambda i: (i % per_seq, 0)
    per_layer = lambda name: _layer_spec(lw[name], layer)
    in_specs = [
        pl.BlockSpec((tm, d), row),
        pl.BlockSpec((None, 1, 6, d), lambda i: (layer, i // per_seq, 0, 0)),
        per_layer("n1g"),
        per_layer("w_in"),
        per_layer("a_v_gain"),
        per_layer("w_s"),
        per_layer("bmat"),
        _const_spec((B_HEADS * B_HEAD_DIM, B_HEADS * B_HEAD_DIM)),
        per_layer("qg"),
        per_layer("kg"),
        pl.BlockSpec((tm, LANE), pos),
        pl.BlockSpec((tm, LANE), pos),
        per_layer("cqg"),
        per_layer("w_uq"),
        per_layer("ckvg"),
        per_layer("w_ukv"),
        pl.BlockSpec((tm, LANE), pos),
        pl.BlockSpec((tm, LANE), pos),
    ]
    out_shapes = (
        jax.ShapeDtypeStruct((t, A_WIDTH), BF16),
        jax.ShapeDtypeStruct((t, N_BRANCH * d), BF16),
        jax.ShapeDtypeStruct((t, B_HEADS * LANE), BF16),
        jax.ShapeDtypeStruct((t, LANE), BF16),
        jax.ShapeDtypeStruct((nb, B_KV_HEADS * B_HEAD_DIM, tm), BF16),
        jax.ShapeDtypeStruct((t, C_HEADS * LANE), BF16),
        jax.ShapeDtypeStruct((t, C_HEADS * LANE), BF16),
        jax.ShapeDtypeStruct((nb, C_HEADS * C_V, tm), BF16),
    )
    out_specs = (
        pl.BlockSpec((tm, A_WIDTH), row),
        pl.BlockSpec((tm, N_BRANCH * d), row),
        pl.BlockSpec((tm, B_HEADS * LANE), row),
        pl.BlockSpec((tm, LANE), row),
        pl.BlockSpec((1, B_KV_HEADS * B_HEAD_DIM, tm), lambda i: (i, 0, 0)),
        pl.BlockSpec((tm, C_HEADS * LANE), row),
        pl.BlockSpec((tm, C_HEADS * LANE), row),
        pl.BlockSpec((1, C_HEADS * C_V, tm), lambda i: (i, 0, 0)),
    )
    return pl.pallas_call(
        _inproj_kernel,
        grid=(nb,),
        in_specs=in_specs,
        out_specs=out_specs,
        out_shape=out_shapes,
        compiler_params=_cparams(("arbitrary",)),
        name="in_proj",
    )(x, mod, lw["n1g"], lw["w_in"], lw["a_v_gain"], lw["w_s"], lw["bmat"], tabs["ones"],
      lw["qg"], lw["kg"], tabs["cosb"], tabs["sinb"], lw["cqg"], lw["w_uq"], lw["ckvg"], lw["w_ukv"],
      tabs["cosc"], tabs["sinc"])


def _attn_kernel(q_ref, k_ref, vt_ref, o_ref, s_ref, *, heads_share_kv, dv):
    tq = q_ref.shape[0]
    n_slabs, _, slab = vt_ref.shape
    tk = s_ref.shape[2]
    group = tk // slab
    n_chunks = n_slabs // group
    ones = jnp.ones((PACKED_ROWS, tk), BF16)
    hs = range(2)
    qs = [q_ref[:, hh * LANE:(hh + 1) * LANE] for hh in hs]

    def scores(c, slot):
        start = pl.multiple_of(c * tk, tk)
        s_t = [lax.dot_general(k_ref[pl.ds(start, tk), (0 if heads_share_kv else hh * LANE):
                                     (LANE if heads_share_kv else (hh + 1) * LANE)],
                               qs[hh], (((1,), (1,)), ((), ())), preferred_element_type=F32) for hh in hs]
        for hh in hs:
            s_ref[slot, hh] = s_t[hh]
        return tuple(jnp.max(s_t[hh], axis=0, keepdims=True) for hh in hs)

    def accumulate(c, slot, state, cmax):
        m_new = [jnp.maximum(state[hh][0], cmax[hh]) for hh in hs]
        p = [jnp.exp2(s_ref[slot, hh] - m_new[hh]).astype(BF16) for hh in hs]
        alpha = [jnp.exp2(state[hh][0] - m_new[hh]) for hh in hs]
        vt = []
        for hh in hs:
            v_rows = slice(0, dv) if heads_share_kv else slice(hh * dv, (hh + 1) * dv)
            v_t = jnp.concatenate([vt_ref[c * group + g, v_rows, :] for g in range(group)], axis=1)
            vt.append(jnp.concatenate([v_t, ones], axis=0))
        pv = [jnp.dot(vt[hh], p[hh], preferred_element_type=F32) for hh in hs]
        return tuple((m_new[hh], state[hh][1] * alpha[hh] + pv[hh]) for hh in hs)

    def body(i, carry):
        state, cmax = carry
        c = 2 * i
        cmax1 = scores(c + 1, 1)
        state = accumulate(c, 0, state, cmax)
        cmax0 = scores(jnp.minimum(c + 2, n_chunks - 1), 0)
        state = accumulate(c + 1, 1, state, cmax1)
        return state, cmax0

    state = tuple((jnp.full((1, tq), -jnp.inf, F32), jnp.zeros((dv + PACKED_ROWS, tq), F32)) for _ in hs)
    cmax = scores(0, 0)
    if n_chunks > 1:
        assert n_chunks % 2 == 0
        state, _ = lax.fori_loop(0, n_chunks // 2, body, (state, cmax))
    else:
        state = accumulate(0, 0, state, cmax)
    outs = [acc[:dv] * (1.0 / acc[dv:dv + 1]) for _, acc in state]
    o_ref[...] = jnp.concatenate(outs, axis=0).T.astype(BF16)


def _attn_call(q, k, vt, batch, seq, tq, heads_share_kv, q_heads_per_kv):
    t = q.shape[0]
    n_heads = q.shape[1] // LANE
    tk = vt.shape[2]
    dv = 64
    n_chunks = seq // tk
    nq = seq // tq
    group = 2 if n_chunks % 4 == 0 else 1
    if heads_share_kv:
        k_spec = pl.BlockSpec((seq, LANE), lambda b, j, i: (b, 0))
        pairs_per_kv = q_heads_per_kv // 2
        vt_spec = pl.BlockSpec((n_chunks, dv, tk), lambda b, j, i: (b, j // pairs_per_kv, 0))
    else:
        k_spec = pl.BlockSpec((seq, 2 * LANE), lambda b, j, i: (b, j))
        vt_spec = pl.BlockSpec((n_chunks, 2 * dv, tk), lambda b, j, i: (b, j, 0))
    return pl.pallas_call(
        functools.partial(_attn_kernel, heads_share_kv=heads_share_kv, dv=dv),
        grid=(batch, n_heads // 2, nq),
        in_specs=[pl.BlockSpec((tq, 2 * LANE), lambda b, j, i: (b * nq + i, j)), k_spec, vt_spec],
        out_specs=pl.BlockSpec((tq, 2 * dv), lambda b, j, i: (b * nq + i, j)),
        out_shape=jax.ShapeDtypeStruct((t, n_heads * dv), BF16),
        compiler_params=_cparams(("arbitrary", "arbitrary", "arbitrary")),
        scratch_shapes=[pltpu.VMEM((2, 2, group * tk, tq), F32)],
        name="attn_shared_kv" if heads_share_kv else "attn_latent",
    )(q, k, vt)


def _merge_kernel(ya_ref, yb_ref, yc_ref, gates_ref, x_ref, mod_ref, n2g_ref, wpa_ref, wpb_ref,
                  wpc_ref, wout_ref, wq_ref, sk_ref,
                  x1_ref, h2_ref, lrow_ref, e1_ref, r2_ref, e2_ref, vals_ref, s_scr):
    d = D_MODEL
    merged = (gates_ref[:, 0:d].astype(F32) * jnp.dot(ya_ref[...], wpa_ref[...], preferred_element_type=F32)
              + gates_ref[:, d:2 * d].astype(F32) * jnp.dot(yb_ref[...], wpb_ref[...], preferred_element_type=F32)
              + gates_ref[:, 2 * d:3 * d].astype(F32) * jnp.dot(yc_ref[...], wpc_ref[...], preferred_element_type=F32))
    x1 = x_ref[...] + mod_ref[0, 2:3, :] * jnp.dot(merged.astype(BF16), wout_ref[...],
                                                   preferred_element_type=F32)
    x1_ref[...] = x1
    h2 = (_rms_rows(x1, n2g_ref[...]) * (1.0 + mod_ref[0, 4:5, :]) + mod_ref[0, 3:4, :]).astype(BF16)
    h2_ref[...] = h2
    qk = jnp.dot(h2, wq_ref[...], preferred_element_type=F32).astype(BF16)

    sub = SUBLANES
    n_tiles = PEER_N_KEYS // sub
    assert n_tiles == PEER_TOPK
    for h in range(PEER_HEADS):
        for p in range(2):
            idx = 2 * h + p
            s = lax.dot_general(sk_ref[idx], qk[:, idx * PEER_HALF:(idx + 1) * PEER_HALF],
                                (((1,), (1,)), ((), ())), preferred_element_type=F32)
            s_scr[p, h] = s
            for t0 in range(0, s.shape[1], LANE):
                tiles = [s[k * sub:(k + 1) * sub, t0:t0 + LANE] for k in range(n_tiles)]
                top = list(tiles)
                for i, j in SORT16:
                    top[i], top[j] = jnp.maximum(top[i], top[j]), jnp.minimum(top[i], top[j])
                for shift in (4, 2, 1):
                    other = [pltpu.roll(v, shift, 0) for v in top]
                    top = [jnp.maximum(top[k], other[n_tiles - 1 - k]) for k in range(n_tiles)]
                    for i, j in BITONIC16:
                        top[i], top[j] = jnp.maximum(top[i], top[j]), jnp.minimum(top[i], top[j])
                for a in range(PEER_TOPK):
                    vals_ref[p, a, h:h + 1, t0:t0 + LANE] = top[a][0:1, :]
                if p == 1:
                    for k in range(0, n_tiles, 2):
                        rank = [functools.reduce(jnp.add, [jnp.where(tiles[k + i] < top[a], 1.0, 0.0)
                                                           for a in range(PEER_TOPK)]) for i in range(2)]
                        r2_ref[h, k * sub:(k + 2) * sub, t0:t0 + LANE] = jnp.concatenate(rank, axis=0).astype(BF16)

    v1 = [vals_ref[0, a] for a in range(PEER_TOPK)]
    v2 = [vals_ref[1, a] for a in range(PEER_TOPK)]
    cands = [v1[a] + v2[b] for a, b in CAND_PAIRS]
    work = list(cands)
    tau = None
    for it in range(PEER_TOPK):
        mx = functools.reduce(jnp.maximum, work)
        if it == PEER_TOPK - 1:
            tau = mx
        else:
            work = [jnp.where(w == mx, -jnp.inf, w) for w in work]
    top = v1[0] + v2[0]
    zsum = functools.reduce(jnp.add, [jnp.where(c >= tau, jnp.exp(c - top), 0.0) for c in cands])
    rz = 1.0 / zsum
    count = [functools.reduce(jnp.add, [jnp.where(c >= tau, 1.0, 0.0)
                                         for c, (a2, _) in zip(cands, CAND_PAIRS) if a2 == a])
             for a in range(PEER_TOPK)]
    for h in range(PEER_HEADS):
        s1 = s_scr[0, h]
        lrow = jnp.zeros(s1.shape, F32)
        for a in range(PEER_TOPK):
            lrow = jnp.where(s1 == v1[a][h:h + 1, :], count[a][h:h + 1, :], lrow)
        lrow_ref[h] = lrow
        e1_ref[h] = jnp.exp(s1 - v1[0][h:h + 1, :])
        e2_ref[h] = (jnp.exp(s_scr[1, h] - v2[0][h:h + 1, :]) * rz[h:h + 1, :]).astype(BF16)


def _merge_call(ya, yb, yc, gates, x, mod, lw, layer, tm, seq):
    t, d = x.shape
    nb = t // tm
    per_seq = seq // tm
    row = lambda i: (i, 0)
    per_layer = lambda name: _layer_spec(lw[name], layer)
    hk = pl.BlockSpec((PEER_HEADS, PEER_N_KEYS, tm), lambda i: (0, 0, i))
    hk_f32 = jax.ShapeDtypeStruct((PEER_HEADS, PEER_N_KEYS, t), F32)
    hk_bf16 = jax.ShapeDtypeStruct((PEER_HEADS, PEER_N_KEYS, t), BF16)
    return pl.pallas_call(
        _merge_kernel,
        grid=(nb,),
        in_specs=[
            pl.BlockSpec((tm, A_WIDTH), row),
            pl.BlockSpec((tm, A_WIDTH), row),
            pl.BlockSpec((tm, A_WIDTH), row),
            pl.BlockSpec((tm, N_BRANCH * d), row),
            pl.BlockSpec((tm, d), row),
            pl.BlockSpec((None, 1, 6, d), lambda i: (layer, i // per_seq, 0, 0)),
            per_layer("n2g"),
            per_layer("w_pa"),
            per_layer("w_pb"),
            per_layer("w_pc"),
            per_layer("w_out"),
            per_layer("w_query"),
            per_layer("sub_keys"),
        ],
        out_specs=(
            pl.BlockSpec((tm, d), row),
            pl.BlockSpec((tm, d), row),
            hk, hk, hk, hk,
        ),
        out_shape=(
            jax.ShapeDtypeStruct((t, d), F32),
            jax.ShapeDtypeStruct((t, d), BF16),
            hk_f32, hk_f32, hk_bf16, hk_bf16,
        ),
        scratch_shapes=[pltpu.VMEM((2, PEER_TOPK, PEER_HEADS, tm), F32),
                        pltpu.VMEM((2, PEER_HEADS, PEER_N_KEYS, tm), F32)],
        compiler_params=_cparams(("arbitrary",)),
        name="merge_route",
    )(ya, yb, yc, gates, x, mod, lw["n2g"], lw["w_pa"], lw["w_pb"], lw["w_pc"], lw["w_out"],
      lw["w_query"], lw["sub_keys"])


def _expert_kernel(h2_ref, lrow_ref, e1_ref, r2_ref, e2_ref, u_ref, v_ref, x1_ref, mod_ref,
                   fg_ref, o_ref, acc_ref, w_ref, *, final_norm):
    ch = pl.program_id(1)
    rows = lrow_ref.shape[1]
    tm = h2_ref.shape[0]
    pack_rows = PACKED_ROWS

    @pl.when(ch == 0)
    def _():
        acc_ref[...] = jnp.zeros_like(acc_ref)

    def row_tile(ref, h, r):
        row = jnp.broadcast_to(ref[h, r:r + 1, :], (pack_rows, tm)).astype(BF16)
        return jnp.concatenate([row] * (PEER_N_KEYS // pack_rows), axis=0)

    zero = jnp.zeros((PEER_N_KEYS, tm), BF16)
    for r in range(rows):
        gate = None
        for h in range(PEER_HEADS):
            sel = jnp.where(r2_ref[h] < row_tile(lrow_ref, h, r), e2_ref[h], zero) * row_tile(e1_ref, h, r)
            gate = sel if gate is None else gate + sel
        rs = slice(r * PEER_N_KEYS, (r + 1) * PEER_N_KEYS)
        w_ref[rs, :] = gate
    act = lax.dot_general(u_ref[...], h2_ref[...], (((1,), (1,)), ((), ())), preferred_element_type=F32)
    for r in range(rows):
        rs = slice(r * PEER_N_KEYS, (r + 1) * PEER_N_KEYS)
        w_ref[rs, :] = w_ref[rs, :] * _gelu(act[rs, :].astype(BF16))
    acc_ref[...] += lax.dot_general(v_ref[...], w_ref[...], (((0,), (0,)), ((), ())),
                                    preferred_element_type=F32)

    @pl.when(ch == pl.num_programs(1) - 1)
    def _():
        x2 = x1_ref[...] + mod_ref[0, 5:6, :] * acc_ref[...].T
        if final_norm:
            x2 = _rms_rows(x2, fg_ref[...])
        o_ref[...] = x2


def _expert_call(h2, lrow, e1, r2, e2, u, v, x1, mod, final_gain, layer, tm, ec, seq, final_norm):
    t, d = x1.shape
    n_exp = u.shape[1]
    rows = ec // PEER_N_KEYS
    per_seq = seq // tm
    tok = lambda i, c: (i, 0)
    chunk_rows = pl.BlockSpec((PEER_HEADS, rows, tm), lambda i, c: (0, c, i))
    all_rows = pl.BlockSpec((PEER_HEADS, PEER_N_KEYS, tm), lambda i, c: (0, 0, i))
    return pl.pallas_call(
        functools.partial(_expert_kernel, final_norm=final_norm),
        grid=(t // tm, n_exp // ec),
        in_specs=[
            pl.BlockSpec((tm, d), tok),
            chunk_rows, chunk_rows, all_rows, all_rows,
            pl.BlockSpec((None, ec, d), lambda i, c: (layer, c, 0)),
            pl.BlockSpec((None, ec, d), lambda i, c: (layer, c, 0)),
            pl.BlockSpec((tm, d), tok),
            pl.BlockSpec((None, 1, 6, d), lambda i, c: (layer, i // per_seq, 0, 0)),
            pl.BlockSpec((1, d), lambda i, c: (0, 0)),
        ],
        out_specs=pl.BlockSpec((tm, d), tok),
        out_shape=jax.ShapeDtypeStruct((t, d), F32),
        scratch_shapes=[pltpu.VMEM((d, tm), F32), pltpu.VMEM((ec, tm), BF16)],
        compiler_params=_cparams(("arbitrary", "arbitrary")),
        name="peer_dense",
    )(h2, lrow, e1, r2, e2, u, v, x1, mod, final_gain)


def _pairs_last(w):
    n = w.shape[-1] // 2
    return jnp.swapaxes(w.reshape(w.shape[:-1] + (n, 2)), -1, -2).reshape(w.shape)


def _reorder_w_in(w):
    lead = w.shape[:-1]
    qw, kw = B_HEADS * B_HEAD_DIM, B_KV_HEADS * B_HEAD_DIM
    tiles = qw // LANE
    a = w[..., :COLS_A]
    q = _pairs_last(w[..., COLS_A:COLS_A + qw].reshape(lead + (2, tiles, B_HEAD_DIM)))
    q = jnp.swapaxes(q, -3, -2).reshape(lead + (qw,))
    k = _pairs_last(w[..., COLS_A + qw:COLS_A + qw + kw].reshape(lead + (B_KV_HEADS, B_HEAD_DIM))).reshape(lead + (kw,))
    v = w[..., COLS_A + qw + kw:COLS_A + COLS_B]
    base_c = COLS_A + COLS_B
    lowrank = w[..., base_c:base_c + C_Q_RANK + C_KV_RANK]
    rope = _pairs_last(w[..., base_c + C_Q_RANK + C_KV_RANK:base_c + COLS_C])
    gates = w[..., base_c + COLS_C:]
    zeros = lambda n: jnp.zeros(lead + (n,), w.dtype)
    return jnp.concatenate([a, q, k, v, lowrank, zeros(C_NOPE), rope, zeros(LANE - C_NOPE - C_ROPE), gates], axis=-1)


def _reorder_w_uq(w):
    lead = w.shape[:-1]
    w = w.reshape(lead + (C_HEADS, C_NOPE + C_ROPE))
    pad = jnp.zeros(lead + (C_HEADS, LANE - C_NOPE - C_ROPE), w.dtype)
    return jnp.concatenate([w[..., :C_NOPE], _pairs_last(w[..., C_NOPE:]), pad], axis=-1).reshape(lead + (C_HEADS * LANE,))


def _reorder_w_ukv(w):
    lead = w.shape[:-1]
    w = w.reshape(lead + (C_HEADS, C_NOPE + C_V))
    pad = jnp.zeros(lead + (C_HEADS, LANE - C_NOPE), w.dtype)
    k = jnp.concatenate([w[..., :C_NOPE], pad], axis=-1).reshape(lead + (C_HEADS * LANE,))
    return jnp.concatenate([k, w[..., C_NOPE:].reshape(lead + (C_HEADS * C_V,))], axis=-1)


def _rope_tables(seq):
    rows = seq // GRID_W
    row = jnp.repeat(jnp.arange(rows, dtype=F32), GRID_W)
    col = jnp.tile(jnp.arange(GRID_W, dtype=F32), rows)

    def cos_sin(d_rot):
        half = d_rot // 2
        freq = ROPE_THETA ** (-jnp.arange(0, half, 2, dtype=F32) / half)
        ang = jnp.concatenate([row[:, None] * freq, col[:, None] * freq], axis=-1)
        return jnp.cos(ang), jnp.sin(ang)

    cb, sb = cos_sin(B_HEAD_DIM)
    cosb = jnp.tile(jnp.concatenate([cb, cb], axis=1), (1, LANE // B_HEAD_DIM))
    sinb = jnp.tile(jnp.concatenate([-sb, sb], axis=1), (1, LANE // B_HEAD_DIM))
    cc, sc = cos_sin(C_ROPE)
    one = jnp.ones((seq, C_NOPE), F32)
    pad = LANE - C_NOPE - C_ROPE
    cosc = jnp.concatenate([one, cc, cc, jnp.ones((seq, pad), F32)], axis=1)
    sinc = jnp.concatenate([0.0 * one, -sc, sc, jnp.zeros((seq, pad), F32)], axis=1)
    return cosb, sinb, cosc, sinc


def kernel(x, c, w_ada, b_ada, norm1_gain, w_in, a_v_gain, a_w_s, a_b_s, b_q_gain, b_k_gain,
           c_q_gain, c_w_uq, c_kv_gain, c_w_ukv, w_pa, w_pb, w_pc, w_out, norm2_gain,
           peer_w_query, peer_sub_keys, peer_u, peer_v, final_gain):
    batch, seq, d = x.shape
    depth = w_ada.shape[0]
    t = batch * seq
    assert d == D_MODEL and seq % GRID_W == 0
    tm, tq, tm_merge, tm_peer, ec = _tile_sizes(seq)

    cosb, sinb, cosc, sinc = _rope_tables(seq)
    qw = B_HEADS * B_HEAD_DIM
    ones = jnp.asarray(np.kron(np.eye(qw // B_HEAD_DIM), np.ones((B_HEAD_DIM, B_HEAD_DIM))), BF16)
    tabs = dict(cosb=cosb, sinb=sinb, cosc=cosc, sinc=sinc, ones=ones)

    assert batch <= SUBLANES and x.dtype == F32 and c.shape == (batch, d)
    c_pad = jnp.zeros((SUBLANES, d), F32).at[:batch].set(c)
    mod_all = _ada_call(c_pad, w_ada, b_ada)[:, :batch, :].reshape(depth, batch, 6, d)

    pair = np.concatenate([np.arange(0, B_HEAD_DIM, 2), np.arange(1, B_HEAD_DIM, 2)])
    lw = dict(
        n1g=norm1_gain[:, None, :],
        w_in=_reorder_w_in(w_in).astype(BF16),
        a_v_gain=a_v_gain[:, None, :],
        w_s=a_w_s.astype(BF16),
        bmat=jnp.repeat(jnp.swapaxes(a_b_s, 1, 2), A_DIM, axis=2),
        qg=jnp.tile(b_q_gain[:, pair], (1, B_HEADS))[:, None, :],
        kg=jnp.tile(b_k_gain[:, pair], (1, B_KV_HEADS))[:, None, :],
        cqg=c_q_gain[:, None, :],
        w_uq=_reorder_w_uq(c_w_uq).astype(BF16),
        ckvg=c_kv_gain[:, None, :],
        w_ukv=_reorder_w_ukv(c_w_ukv).astype(BF16),
        n2g=norm2_gain[:, None, :],
        w_pa=w_pa.astype(BF16),
        w_pb=w_pb.astype(BF16),
        w_pc=w_pc.astype(BF16),
        w_out=w_out.astype(BF16),
        w_query=peer_w_query.astype(BF16),
        sub_keys=peer_sub_keys.reshape(depth, 2 * PEER_HEADS, PEER_N_KEYS, PEER_HALF).astype(BF16),
    )
    u_b = peer_u.astype(BF16)
    v_b = peer_v.astype(BF16)

    xf = x.reshape(t, d)
    for l in range(depth):
        ya, gates, qb, kb, vtb, qc, kc, vtc = _inproj_call(xf, mod_all, lw, tabs, l, tm, seq)
        yb = _attn_call(qb, kb, vtb, batch, seq, tq, True, B_HEADS // B_KV_HEADS)
        yc = _attn_call(qc, kc, vtc, batch, seq, tq, False, 1)
        x1, h2, lrow, e1, r2, e2 = _merge_call(ya, yb, yc, gates, xf, mod_all, lw, l, tm_merge, seq)
        xf = _expert_call(h2, lrow, e1, r2, e2, u_b, v_b, x1, mod_all, final_gain[None, :],
                          l, tm_peer, ec, seq, l == depth - 1)
    return xf.reshape(batch, seq, d)
```

```python
import functools

import numpy as np
import jax
import jax.numpy as jnp
from jax import lax
from jax.experimental import pallas as pl
from jax.experimental.pallas import tpu as pltpu

F32 = jnp.float32
BF16 = jnp.bfloat16

LANE = 128
SUBLANES = 8
PACKED_ROWS = 16
VMEM_LIMIT = 56 * 1024 * 1024

D_MODEL = 1024
GRID_W = 64
BLOCK = 128
EPS = 1e-6
ROPE_THETA = 10000.0
LOG2E = 1.4426950408889634

A_GROUPS = 8
A_DIM = 64
A_WIDTH = A_GROUPS * A_DIM
B_HEADS = 8
B_KV_HEADS = 2
B_HEAD_DIM = 64
C_HEADS = 8
C_NOPE = 64
C_ROPE = 32
C_V = 64
C_Q_RANK = 256
C_KV_RANK = 128
COLS_A = 2 * A_WIDTH
COLS_B = (B_HEADS + 2 * B_KV_HEADS) * B_HEAD_DIM
COLS_C = C_Q_RANK + C_KV_RANK + C_ROPE
N_BRANCH = 3
PEER_HEADS = 8
PEER_N_KEYS = 128
PEER_HALF = 128
PEER_TOPK = 16

OFF_A = 0
OFF_B = OFF_A + COLS_A
OFF_C = OFF_B + COLS_B
OFF_G = OFF_C + C_Q_RANK + C_KV_RANK + LANE
W_IN_COLS = OFF_G + N_BRANCH * D_MODEL

CAND_PAIRS = [(a, b) for a in range(PEER_TOPK) for b in range(PEER_TOPK) if (a + 1) * (b + 1) <= PEER_TOPK]


def _bitonic_merge_pairs(n):
    pairs, stride = [], n // 2
    while stride:
        pairs += [(i, i + stride) for i in range(n) if not i & stride]
        stride //= 2
    return pairs


def _sort_pairs(n):
    pairs = []

    def merge(lo, cnt, step):
        nxt = step * 2
        if nxt < cnt:
            merge(lo, cnt, nxt)
            merge(lo + step, cnt, nxt)
            pairs.extend((i, i + step) for i in range(lo + step, lo + cnt - step, nxt))
        else:
            pairs.append((lo, lo + step))

    def sort(lo, cnt):
        if cnt > 1:
            half = cnt // 2
            sort(lo, half)
            sort(lo + half, half)
            merge(lo, cnt, 1)

    sort(0, n)
    return pairs


SORT16 = _sort_pairs(PEER_TOPK)
BITONIC16 = _bitonic_merge_pairs(PEER_TOPK)


def _tile_sizes(seq):
    tm = min(512, seq)
    tq = min(1024, seq)
    tm_merge = min(256, seq)
    tm_peer = min(1024, seq)
    ec = 8 * PEER_N_KEYS
    assert seq % tm == 0 and seq % tq == 0 and seq % tm_peer == 0 and seq % tm_merge == 0
    return tm, tq, tm_merge, tm_peer, ec


def _cparams(sem):
    return pltpu.CompilerParams(dimension_semantics=sem, vmem_limit_bytes=VMEM_LIMIT)


def _const_spec(shape):
    nd = len(shape)
    return pl.BlockSpec(shape, lambda *_: (0,) * nd, pipeline_mode=pl.Buffered(1))


def _layer_spec(stacked, layer):
    shape = stacked.shape[1:]
    return pl.BlockSpec((None,) + shape, lambda *_: (layer,) + (0,) * len(shape), pipeline_mode=pl.Buffered(1))


def _gelu(x):
    return 0.5 * x * (1.0 + jnp.tanh(0.7978845608028654 * (x + 0.044715 * (x * x * x))))


def _rms_rows(x, gain):
    return x * lax.rsqrt(jnp.mean(x * x, axis=-1, keepdims=True) + EPS) * gain


def _group_sumsq(x, ones_blockdiag):
    sq = x * x
    hi = sq.astype(BF16)
    lo = (sq - hi.astype(F32)).astype(BF16)
    return (jnp.dot(hi, ones_blockdiag, preferred_element_type=F32)
            + jnp.dot(lo, ones_blockdiag, preferred_element_type=F32))


def _swap_halves(x, group, lo_mask):
    w = x.shape[-1]
    half = group // 2
    return jnp.where(lo_mask, pltpu.roll(x, w - half, 1), pltpu.roll(x, half, 1))


def _ada_kernel(c_ref, w_ref, b_ref, o_ref):
    c = c_ref[...]
    ca = c * jax.nn.sigmoid(c)
    o_ref[0] = jnp.dot(ca, w_ref[0], preferred_element_type=F32,
                       precision=lax.Precision.HIGHEST) + b_ref[0]


def _ada_call(c_pad, w_ada, b_ada):
    depth, d, n = w_ada.shape
    tn = n // 4
    return pl.pallas_call(
        _ada_kernel,
        grid=(depth, n // tn),
        in_specs=[
            pl.BlockSpec((SUBLANES, d), lambda l, j: (0, 0)),
            pl.BlockSpec((1, d, tn), lambda l, j: (l, 0, j)),
            pl.BlockSpec((1, 1, tn), lambda l, j: (l, 0, j)),
        ],
        out_specs=pl.BlockSpec((1, SUBLANES, tn), lambda l, j: (l, 0, j)),
        out_shape=jax.ShapeDtypeStruct((depth, SUBLANES, n), F32),
        compiler_params=_cparams(("arbitrary", "arbitrary")),
        name="ada_mod",
    )(c_pad, w_ada, b_ada.reshape(depth, 1, n))


def _inproj_kernel(x_ref, mod_ref, n1g_ref, win_ref, avg_ref, ws_ref, bmat_ref, ones_ref,
                   qg_ref, kg_ref, cosb_ref, sinb_ref, cqg_ref, wuq_ref, ckvg_ref, wukv_ref,
                   cosc_ref, sinc_ref,
                   ya_ref, gates_ref, qb_ref, kb_ref, vtb_ref, qc_ref, kc_ref, vtc_ref):
    tm = x_ref.shape[0]
    d = D_MODEL
    shift = mod_ref[0, 0:1, :]
    scale = mod_ref[0, 1:2, :]
    h = (_rms_rows(x_ref[...], n1g_ref[...]) * (1.0 + scale) + shift).astype(BF16)

    za = jnp.dot(h, win_ref[:, OFF_A:OFF_A + COLS_A], preferred_element_type=F32)
    z = _gelu(za)
    u = z[:, :A_WIDTH]
    vn = _rms_rows(z[:, A_WIDTH:], avg_ref[...]).astype(BF16)
    grp = lax.broadcasted_iota(jnp.int32, (BLOCK, A_WIDTH), 1) // A_DIM
    for c in range(tm // BLOCK):
        vc = vn[c * BLOCK:(c + 1) * BLOCK]
        mixed = jnp.zeros((BLOCK, A_WIDTH), F32)
        for g in range(A_GROUPS):
            r = jnp.dot(ws_ref[g], vc, preferred_element_type=F32)
            mixed = jnp.where(grp == g, r, mixed)
        ya_ref[c * BLOCK:(c + 1) * BLOCK, :] = (
            u[c * BLOCK:(c + 1) * BLOCK] * (mixed + bmat_ref[...])).astype(BF16)

    zb = jnp.dot(h, win_ref[:, OFF_B:OFF_B + COLS_B], preferred_element_type=F32)
    qw = B_HEADS * B_HEAD_DIM
    kw = B_KV_HEADS * B_HEAD_DIM
    cosb = cosb_ref[...]
    sinb = sinb_ref[...]
    lane_q = lax.broadcasted_iota(jnp.int32, (tm, qw), 1)
    q = zb[:, :qw]
    qn = q * lax.rsqrt(_group_sumsq(q, ones_ref[...]) * (1.0 / B_HEAD_DIM) + EPS) * qg_ref[...]
    reps = qw // LANE
    qr = (qn * jnp.concatenate([cosb] * reps, axis=1)
          + _swap_halves(qn, B_HEAD_DIM, (lane_q % B_HEAD_DIM) < B_HEAD_DIM // 2)
          * jnp.concatenate([sinb] * reps, axis=1)) * (B_HEAD_DIM ** -0.5 * LOG2E)
    lane_t = lax.broadcasted_iota(jnp.int32, (tm, LANE), 1)
    for t in range(reps):
        tile = qr[:, t * LANE:(t + 1) * LANE]
        qb_ref[:, t * LANE:(t + 1) * LANE] = jnp.where(lane_t < B_HEAD_DIM, tile, 0.0).astype(BF16)
        qb_ref[:, (reps + t) * LANE:(reps + t + 1) * LANE] = (
            jnp.where(lane_t >= B_HEAD_DIM, tile, 0.0).astype(BF16))
    k = zb[:, qw:qw + kw]
    kn = k * lax.rsqrt(_group_sumsq(k, ones_ref[0:kw, 0:kw]) * (1.0 / B_HEAD_DIM) + EPS) * kg_ref[...]
    kr = kn * cosb + _swap_halves(kn, B_HEAD_DIM, (lane_t % B_HEAD_DIM) < B_HEAD_DIM // 2) * sinb
    kb_ref[...] = kr.astype(BF16)
    vtb_ref[0] = zb[:, qw + kw:].T.astype(BF16)

    zc = jnp.dot(h, win_ref[:, OFF_C:OFF_G], preferred_element_type=F32)
    cosc = cosc_ref[...]
    sinc = sinc_ref[...]
    rope_lo = C_NOPE + C_ROPE // 2
    cqn = _rms_rows(zc[:, :C_Q_RANK], cqg_ref[...]).astype(BF16)
    qf = jnp.dot(cqn, wuq_ref[...], preferred_element_type=F32)
    lane_c = lax.broadcasted_iota(jnp.int32, qf.shape, 1)
    qrot = (qf * jnp.concatenate([cosc] * C_HEADS, axis=1)
            + _swap_halves(qf, C_ROPE, (lane_c % LANE) < rope_lo)
            * jnp.concatenate([sinc] * C_HEADS, axis=1)) * ((C_NOPE + C_ROPE) ** -0.5 * LOG2E)
    qc_ref[...] = qrot.astype(BF16)
    ckvn = _rms_rows(zc[:, C_Q_RANK:C_Q_RANK + C_KV_RANK], ckvg_ref[...]).astype(BF16)
    kv = jnp.dot(ckvn, wukv_ref[...], preferred_element_type=F32)
    krope = zc[:, C_Q_RANK + C_KV_RANK:]
    krot = krope * cosc + _swap_halves(krope, C_ROPE, lane_t < rope_lo) * sinc
    kc_ref[...] = (kv[:, :C_HEADS * LANE] + jnp.concatenate([krot] * C_HEADS, axis=1)).astype(BF16)
    vtc_ref[0] = kv[:, C_HEADS * LANE:].T.astype(BF16)

    for n in range(N_BRANCH):
        zg = jnp.dot(h, win_ref[:, OFF_G + n * d:OFF_G + (n + 1) * d], preferred_element_type=F32)
        gates_ref[:, n * d:(n + 1) * d] = jax.nn.sigmoid(zg).astype(BF16)


def _inproj_call(x, mod, lw, tabs, layer, tm, seq):
    t, d = x.shape
    nb = t // tm
    per_seq = seq // tm
    row = lambda i: (i, 0)
    pos = lambda i: (i % per_seq, 0)
    per_layer = lambda name: _layer_spec(lw[name], layer)
    in_specs = [
        pl.BlockSpec((tm, d), row),
        pl.BlockSpec((None, 1, 6, d), lambda i: (layer, i // per_seq, 0, 0)),
        per_layer("n1g"),
        per_layer("w_in"),
        per_layer("a_v_gain"),
        per_layer("w_s"),
        per_layer("bmat"),
        _const_spec((B_HEADS * B_HEAD_DIM, B_HEADS * B_HEAD_DIM)),
        per_layer("qg"),
        per_layer("kg"),
        pl.BlockSpec((tm, LANE), pos),
        pl.BlockSpec((tm, LANE), pos),
        per_layer("cqg"),
        per_layer("w_uq"),
        per_layer("ckvg"),
        per_layer("w_ukv"),
        pl.BlockSpec((tm, LANE), pos),
        pl.BlockSpec((tm, LANE), pos),
    ]
    out_shapes = (
        jax.ShapeDtypeStruct((t, A_WIDTH), BF16),
        jax.ShapeDtypeStruct((t, N_BRANCH * d), BF16),
        jax.ShapeDtypeStruct((t, B_HEADS * LANE), BF16),
        jax.ShapeDtypeStruct((t, LANE), BF16),
        jax.ShapeDtypeStruct((nb, B_KV_HEADS * B_HEAD_DIM, tm), BF16),
        jax.ShapeDtypeStruct((t, C_HEADS * LANE), BF16),
        jax.ShapeDtypeStruct((t, C_HEADS * LANE), BF16),
        jax.ShapeDtypeStruct((nb, C_HEADS * C_V, tm), BF16),
    )
    out_specs = (
        pl.BlockSpec((tm, A_WIDTH), row),
        pl.BlockSpec((tm, N_BRANCH * d), row),
        pl.BlockSpec((tm, B_HEADS * LANE), row),
        pl.BlockSpec((tm, LANE), row),
        pl.BlockSpec((1, B_KV_HEADS * B_HEAD_DIM, tm), lambda i: (i, 0, 0)),
        pl.BlockSpec((tm, C_HEADS * LANE), row),
        pl.BlockSpec((tm, C_HEADS * LANE), row),
        pl.BlockSpec((1, C_HEADS * C_V, tm), lambda i: (i, 0, 0)),
    )
    return pl.pallas_call(
        _inproj_kernel,
        grid=(nb,),
        in_specs=in_specs,
        out_specs=out_specs,
        out_shape=out_shapes,
        compiler_params=_cparams(("arbitrary",)),
        name="in_proj",
    )(x, mod, lw["n1g"], lw["w_in"], lw["a_v_gain"], lw["w_s"], lw["bmat"], tabs["ones"],
      lw["qg"], lw["kg"], tabs["cosb"], tabs["sinb"], lw["cqg"], lw["w_uq"], lw["ckvg"], lw["w_ukv"],
      tabs["cosc"], tabs["sinc"])


def _attn_kernel(q_ref, k_ref, vt_ref, o_ref, s_ref, *, heads_share_kv, dv):
    tq = q_ref.shape[0]
    n_slabs, _, slab = vt_ref.shape
    tk = s_ref.shape[2]
    group = tk // slab
    n_chunks = n_slabs // group
    ones = jnp.ones((PACKED_ROWS, tk), BF16)
    hs = range(2)
    qs = [q_ref[:, hh * LANE:(hh + 1) * LANE] for hh in hs]

    def scores(c, slot):
        start = pl.multiple_of(c * tk, tk)
        s_t = [lax.dot_general(k_ref[pl.ds(start, tk), (0 if heads_share_kv else hh * LANE):
                                     (LANE if heads_share_kv else (hh + 1) * LANE)],
                               qs[hh], (((1,), (1,)), ((), ())), preferred_element_type=F32) for hh in hs]
        for hh in hs:
            s_ref[slot, hh] = s_t[hh]
        return tuple(jnp.max(s_t[hh], axis=0, keepdims=True) for hh in hs)

    def accumulate(c, slot, state, cmax):
        m_new = [jnp.maximum(state[hh][0], cmax[hh]) for hh in hs]
        p = [jnp.exp2(s_ref[slot, hh] - m_new[hh]).astype(BF16) for hh in hs]
        alpha = [jnp.exp2(state[hh][0] - m_new[hh]) for hh in hs]
        vt = []
        for hh in hs:
            v_rows = slice(0, dv) if heads_share_kv else slice(hh * dv, (hh + 1) * dv)
            v_t = jnp.concatenate([vt_ref[c * group + g, v_rows, :] for g in range(group)], axis=1)
            vt.append(jnp.concatenate([v_t, ones], axis=0))
        pv = [jnp.dot(vt[hh], p[hh], preferred_element_type=F32) for hh in hs]
        return tuple((m_new[hh], state[hh][1] * alpha[hh] + pv[hh]) for hh in hs)

    def body(i, carry):
        state, cmax = carry
        c = 2 * i
        cmax1 = scores(c + 1, 1)
        state = accumulate(c, 0, state, cmax)
        cmax0 = scores(jnp.minimum(c + 2, n_chunks - 1), 0)
        state = accumulate(c + 1, 1, state, cmax1)
        return state, cmax0

    state = tuple((jnp.full((1, tq), -jnp.inf, F32), jnp.zeros((dv + PACKED_ROWS, tq), F32)) for _ in hs)
    cmax = scores(0, 0)
    if n_chunks > 1:
        assert n_chunks % 2 == 0
        state, _ = lax.fori_loop(0, n_chunks // 2, body, (state, cmax))
    else:
        state = accumulate(0, 0, state, cmax)
    outs = [acc[:dv] * (1.0 / acc[dv:dv + 1]) for _, acc in state]
    o_ref[...] = jnp.concatenate(outs, axis=0).T.astype(BF16)


def _attn_call(q, k, vt, batch, seq, tq, heads_share_kv, q_heads_per_kv):
    t = q.shape[0]
    n_heads = q.shape[1] // LANE
    tk = vt.shape[2]
    dv = 64
    n_chunks = seq // tk
    nq = seq // tq
    group = 2 if n_chunks % 4 == 0 else 1
    if heads_share_kv:
        k_spec = pl.BlockSpec((seq, LANE), lambda b, j, i: (b, 0))
        pairs_per_kv = q_heads_per_kv // 2
        vt_spec = pl.BlockSpec((n_chunks, dv, tk), lambda b, j, i: (b, j // pairs_per_kv, 0))
    else:
        k_spec = pl.BlockSpec((seq, 2 * LANE), lambda b, j, i: (b, j))
        vt_spec = pl.BlockSpec((n_chunks, 2 * dv, tk), lambda b, j, i: (b, j, 0))
    return pl.pallas_call(
        functools.partial(_attn_kernel, heads_share_kv=heads_share_kv, dv=dv),
        grid=(batch, n_heads // 2, nq),
        in_specs=[pl.BlockSpec((tq, 2 * LANE), lambda b, j, i: (b * nq + i, j)), k_spec, vt_spec],
        out_specs=pl.BlockSpec((tq, 2 * dv), lambda b, j, i: (b * nq + i, j)),
        out_shape=jax.ShapeDtypeStruct((t, n_heads * dv), BF16),
        compiler_params=_cparams(("arbitrary", "arbitrary", "arbitrary")),
        scratch_shapes=[pltpu.VMEM((2, 2, group * tk, tq), F32)],
        name="attn_shared_kv" if heads_share_kv else "attn_latent",
    )(q, k, vt)


def _merge_kernel(ya_ref, yb_ref, yc_ref, gates_ref, x_ref, mod_ref, n2g_ref, wpa_ref, wpb_ref,
                  wpc_ref, wout_ref, wq_ref, sk_ref,
                  x1_ref, h2_ref, lrow_ref, e1_ref, r2_ref, e2_ref, vals_ref, s_scr):
    d = D_MODEL
    merged = (gates_ref[:, 0:d].astype(F32) * jnp.dot(ya_ref[...], wpa_ref[...], preferred_element_type=F32)
              + gates_ref[:, d:2 * d].astype(F32) * jnp.dot(yb_ref[...], wpb_ref[...], preferred_element_type=F32)
              + gates_ref[:, 2 * d:3 * d].astype(F32) * jnp.dot(yc_ref[...], wpc_ref[...], preferred_element_type=F32))
    x1 = x_ref[...] + mod_ref[0, 2:3, :] * jnp.dot(merged.astype(BF16), wout_ref[...],
                                                   preferred_element_type=F32)
    x1_ref[...] = x1
    h2 = (_rms_rows(x1, n2g_ref[...]) * (1.0 + mod_ref[0, 4:5, :]) + mod_ref[0, 3:4, :]).astype(BF16)
    h2_ref[...] = h2
    qk = jnp.dot(h2, wq_ref[...], preferred_element_type=F32).astype(BF16)

    sub = SUBLANES
    n_tiles = PEER_N_KEYS // sub
    assert n_tiles == PEER_TOPK
    for h in range(PEER_HEADS):
        for p in range(2):
            idx = 2 * h + p
            s = lax.dot_general(sk_ref[idx], qk[:, idx * PEER_HALF:(idx + 1) * PEER_HALF],
                                (((1,), (1,)), ((), ())), preferred_element_type=F32)
            s_scr[p, h] = s
            for t0 in range(0, s.shape[1], LANE):
                tiles = [s[k * sub:(k + 1) * sub, t0:t0 + LANE] for k in range(n_tiles)]
                top = list(tiles)
                for i, j in SORT16:
                    top[i], top[j] = jnp.maximum(top[i], top[j]), jnp.minimum(top[i], top[j])
                for shift in (4, 2, 1):
                    other = [pltpu.roll(v, shift, 0) for v in top]
                    top = [jnp.maximum(top[k], other[n_tiles - 1 - k]) for k in range(n_tiles)]
                    for i, j in BITONIC16:
                        top[i], top[j] = jnp.maximum(top[i], top[j]), jnp.minimum(top[i], top[j])
                for a in range(PEER_TOPK):
                    vals_ref[p, a, h:h + 1, t0:t0 + LANE] = top[a][0:1, :]
                if p == 1:
                    for k in range(0, n_tiles, 2):
                        rank = [functools.reduce(jnp.add, [jnp.where(tiles[k + i] < top[a], 1.0, 0.0)
                                                           for a in range(PEER_TOPK)]) for i in range(2)]
                        r2_ref[h, k * sub:(k + 2) * sub, t0:t0 + LANE] = jnp.concatenate(rank, axis=0).astype(BF16)

    v1 = [vals_ref[0, a] for a in range(PEER_TOPK)]
    v2 = [vals_ref[1, a] for a in range(PEER_TOPK)]
    cands = [v1[a] + v2[b] for a, b in CAND_PAIRS]
    work = list(cands)
    tau = None
    for it in range(PEER_TOPK):
        mx = functools.reduce(jnp.maximum, work)
        if it == PEER_TOPK - 1:
            tau = mx
        else:
            work = [jnp.where(w == mx, -jnp.inf, w) for w in work]
    top = v1[0] + v2[0]
    zsum = functools.reduce(jnp.add, [jnp.where(c >= tau, jnp.exp(c - top), 0.0) for c in cands])
    rz = 1.0 / zsum
    count = [functools.reduce(jnp.add, [jnp.where(c >= tau, 1.0, 0.0)
                                         for c, (a2, _) in zip(cands, CAND_PAIRS) if a2 == a])
             for a in range(PEER_TOPK)]
    for h in range(PEER_HEADS):
        s1 = s_scr[0, h]
        lrow = jnp.zeros(s1.shape, F32)
        for a in range(PEER_TOPK):
            lrow = jnp.where(s1 == v1[a][h:h + 1, :], count[a][h:h + 1, :], lrow)
        lrow_ref[h] = lrow
        e1_ref[h] = jnp.exp(s1 - v1[0][h:h + 1, :])
        e2_ref[h] = (jnp.exp(s_scr[1, h] - v2[0][h:h + 1, :]) * rz[h:h + 1, :]).astype(BF16)


def _merge_call(ya, yb, yc, gates, x, mod, lw, layer, tm, seq):
    t, d = x.shape
    nb = t // tm
    per_seq = seq // tm
    row = lambda i: (i, 0)
    per_layer = lambda name: _layer_spec(lw[name], layer)
    hk = pl.BlockSpec((PEER_HEADS, PEER_N_KEYS, tm), lambda i: (0, 0, i))
    hk_f32 = jax.ShapeDtypeStruct((PEER_HEADS, PEER_N_KEYS, t), F32)
    hk_bf16 = jax.ShapeDtypeStruct((PEER_HEADS, PEER_N_KEYS, t), BF16)
    return pl.pallas_call(
        _merge_kernel,
        grid=(nb,),
        in_specs=[
            pl.BlockSpec((tm, A_WIDTH), row),
            pl.BlockSpec((tm, A_WIDTH), row),
            pl.BlockSpec((tm, A_WIDTH), row),
            pl.BlockSpec((tm, N_BRANCH * d), row),
            pl.BlockSpec((tm, d), row),
            pl.BlockSpec((None, 1, 6, d), lambda i: (layer, i // per_seq, 0, 0)),
            per_layer("n2g"),
            per_layer("w_pa"),
            per_layer("w_pb"),
            per_layer("w_pc"),
            per_layer("w_out"),
            per_layer("w_query"),
            per_layer("sub_keys"),
        ],
        out_specs=(
            pl.BlockSpec((tm, d), row),
            pl.BlockSpec((tm, d), row),
            hk, hk, hk, hk,
        ),
        out_shape=(
            jax.ShapeDtypeStruct((t, d), F32),
            jax.ShapeDtypeStruct((t, d), BF16),
            hk_f32, hk_f32, hk_bf16, hk_bf16,
        ),
        scratch_shapes=[pltpu.VMEM((2, PEER_TOPK, PEER_HEADS, tm), F32),
                        pltpu.VMEM((2, PEER_HEADS, PEER_N_KEYS, tm), F32)],
        compiler_params=_cparams(("arbitrary",)),
        name="merge_route",
    )(ya, yb, yc, gates, x, mod, lw["n2g"], lw["w_pa"], lw["w_pb"], lw["w_pc"], lw["w_out"],
      lw["w_query"], lw["sub_keys"])


def _expert_kernel(h2_ref, lrow_ref, e1_ref, r2_ref, e2_ref, u_ref, v_ref, x1_ref, mod_ref,
                   fg_ref, o_ref, acc_ref, w_ref, *, final_norm):
    ch = pl.program_id(1)
    rows = lrow_ref.shape[1]
    tm = h2_ref.shape[0]
    pack_rows = PACKED_ROWS

    @pl.when(ch == 0)
    def _():
        acc_ref[...] = jnp.zeros_like(acc_ref)

    def row_tile(ref, h, r):
        row = jnp.broadcast_to(ref[h, r:r + 1, :], (pack_rows, tm)).astype(BF16)
        return jnp.concatenate([row] * (PEER_N_KEYS // pack_rows), axis=0)

    for r in range(rows):
        gate = None
        for h in range(PEER_HEADS):
            sel = jnp.where(r2_ref[h] < row_tile(lrow_ref, h, r), e2_ref[h] * row_tile(e1_ref, h, r), 0.0)
            gate = sel if gate is None else gate + sel
        rs = slice(r * PEER_N_KEYS, (r + 1) * PEER_N_KEYS)
        w_ref[rs, :] = gate
    act = lax.dot_general(u_ref[...], h2_ref[...], (((1,), (1,)), ((), ())), preferred_element_type=F32)
    for r in range(rows):
        rs = slice(r * PEER_N_KEYS, (r + 1) * PEER_N_KEYS)
        w_ref[rs, :] = w_ref[rs, :] * _gelu(act[rs, :].astype(BF16))
    acc_ref[...] += lax.dot_general(v_ref[...], w_ref[...], (((0,), (0,)), ((), ())),
                                    preferred_element_type=F32)

    @pl.when(ch == pl.num_programs(1) - 1)
    def _():
        x2 = x1_ref[...] + mod_ref[0, 5:6, :] * acc_ref[...].T
        if final_norm:
            x2 = _rms_rows(x2, fg_ref[...])
        o_ref[...] = x2


def _expert_call(h2, lrow, e1, r2, e2, u, v, x1, mod, final_gain, layer, tm, ec, seq, final_norm):
    t, d = x1.shape
    n_exp = u.shape[1]
    rows = ec // PEER_N_KEYS
    per_seq = seq // tm
    tok = lambda i, c: (i, 0)
    chunk_rows = pl.BlockSpec((PEER_HEADS, rows, tm), lambda i, c: (0, c, i))
    all_rows = pl.BlockSpec((PEER_HEADS, PEER_N_KEYS, tm), lambda i, c: (0, 0, i))
    return pl.pallas_call(
        functools.partial(_expert_kernel, final_norm=final_norm),
        grid=(t // tm, n_exp // ec),
        in_specs=[
            pl.BlockSpec((tm, d), tok),
            chunk_rows, chunk_rows, all_rows, all_rows,
            pl.BlockSpec((None, ec, d), lambda i, c: (layer, c, 0)),
            pl.BlockSpec((None, ec, d), lambda i, c: (layer, c, 0)),
            pl.BlockSpec((tm, d), tok),
            pl.BlockSpec((None, 1, 6, d), lambda i, c: (layer, i // per_seq, 0, 0)),
            pl.BlockSpec((1, d), lambda i, c: (0, 0)),
        ],
        out_specs=pl.BlockSpec((tm, d), tok),
        out_shape=jax.ShapeDtypeStruct((t, d), F32),
        scratch_shapes=[pltpu.VMEM((d, tm), F32), pltpu.VMEM((ec, tm), BF16)],
        compiler_params=_cparams(("arbitrary", "arbitrary")),
        name="peer_dense",
    )(h2, lrow, e1, r2, e2, u, v, x1, mod, final_gain)


def _pairs_last(w):
    n = w.shape[-1] // 2
    return jnp.swapaxes(w.reshape(w.shape[:-1] + (n, 2)), -1, -2).reshape(w.shape)


def _reorder_w_in(w):
    lead = w.shape[:-1]
    qw, kw = B_HEADS * B_HEAD_DIM, B_KV_HEADS * B_HEAD_DIM
    tiles = qw // LANE
    a = w[..., :COLS_A]
    q = _pairs_last(w[..., COLS_A:COLS_A + qw].reshape(lead + (2, tiles, B_HEAD_DIM)))
    q = jnp.swapaxes(q, -3, -2).reshape(lead + (qw,))
    k = _pairs_last(w[..., COLS_A + qw:COLS_A + qw + kw].reshape(lead + (B_KV_HEADS, B_HEAD_DIM))).reshape(lead + (kw,))
    v = w[..., COLS_A + qw + kw:COLS_A + COLS_B]
    base_c = COLS_A + COLS_B
    lowrank = w[..., base_c:base_c + C_Q_RANK + C_KV_RANK]
    rope = _pairs_last(w[..., base_c + C_Q_RANK + C_KV_RANK:base_c + COLS_C])
    gates = w[..., base_c + COLS_C:]
    zeros = lambda n: jnp.zeros(lead + (n,), w.dtype)
    return jnp.concatenate([a, q, k, v, lowrank, zeros(C_NOPE), rope, zeros(LANE - C_NOPE - C_ROPE), gates], axis=-1)


def _reorder_w_uq(w):
    lead = w.shape[:-1]
    w = w.reshape(lead + (C_HEADS, C_NOPE + C_ROPE))
    pad = jnp.zeros(lead + (C_HEADS, LANE - C_NOPE - C_ROPE), w.dtype)
    return jnp.concatenate([w[..., :C_NOPE], _pairs_last(w[..., C_NOPE:]), pad], axis=-1).reshape(lead + (C_HEADS * LANE,))


def _reorder_w_ukv(w):
    lead = w.shape[:-1]
    w = w.reshape(lead + (C_HEADS, C_NOPE + C_V))
    pad = jnp.zeros(lead + (C_HEADS, LANE - C_NOPE), w.dtype)
    k = jnp.concatenate([w[..., :C_NOPE], pad], axis=-1).reshape(lead + (C_HEADS * LANE,))
    return jnp.concatenate([k, w[..., C_NOPE:].reshape(lead + (C_HEADS * C_V,))], axis=-1)


def _rope_tables(seq):
    rows = seq // GRID_W
    row = jnp.repeat(jnp.arange(rows, dtype=F32), GRID_W)
    col = jnp.tile(jnp.arange(GRID_W, dtype=F32), rows)

    def cos_sin(d_rot):
        half = d_rot // 2
        freq = ROPE_THETA ** (-jnp.arange(0, half, 2, dtype=F32) / half)
        ang = jnp.concatenate([row[:, None] * freq, col[:, None] * freq], axis=-1)
        return jnp.cos(ang), jnp.sin(ang)

    cb, sb = cos_sin(B_HEAD_DIM)
    cosb = jnp.tile(jnp.concatenate([cb, cb], axis=1), (1, LANE // B_HEAD_DIM))
    sinb = jnp.tile(jnp.concatenate([-sb, sb], axis=1), (1, LANE // B_HEAD_DIM))
    cc, sc = cos_sin(C_ROPE)
    one = jnp.ones((seq, C_NOPE), F32)
    pad = LANE - C_NOPE - C_ROPE
    cosc = jnp.concatenate([one, cc, cc, jnp.ones((seq, pad), F32)], axis=1)
    sinc = jnp.concatenate([0.0 * one, -sc, sc, jnp.zeros((seq, pad), F32)], axis=1)
    return cosb, sinb, cosc, sinc


def kernel(x, c, w_ada, b_ada, norm1_gain, w_in, a_v_gain, a_w_s, a_b_s, b_q_gain, b_k_gain,
           c_q_gain, c_w_uq, c_kv_gain, c_w_ukv, w_pa, w_pb, w_pc, w_out, norm2_gain,
           peer_w_query, peer_sub_keys, peer_u, peer_v, final_gain):
    batch, seq, d = x.shape
    depth = w_ada.shape[0]
    t = batch * seq
    assert d == D_MODEL and seq % GRID_W == 0
    tm, tq, tm_merge, tm_peer, ec = _tile_sizes(seq)

    cosb, sinb, cosc, sinc = _rope_tables(seq)
    qw = B_HEADS * B_HEAD_DIM
    ones = jnp.asarray(np.kron(np.eye(qw // B_HEAD_DIM), np.ones((B_HEAD_DIM, B_HEAD_DIM))), BF16)
    tabs = dict(cosb=cosb, sinb=sinb, cosc=cosc, sinc=sinc, ones=ones)

    assert batch <= SUBLANES and x.dtype == F32 and c.shape == (batch, d)
    c_pad = jnp.zeros((SUBLANES, d), F32).at[:batch].set(c)
    mod_all = _ada_call(c_pad, w_ada, b_ada)[:, :batch, :].reshape(depth, batch, 6, d)

    pair = np.concatenate([np.arange(0, B_HEAD_DIM, 2), np.arange(1, B_HEAD_DIM, 2)])
    lw = dict(
        n1g=norm1_gain[:, None, :],
        w_in=_reorder_w_in(w_in).astype(BF16),
        a_v_gain=a_v_gain[:, None, :],
        w_s=a_w_s.astype(BF16),
        bmat=jnp.repeat(jnp.swapaxes(a_b_s, 1, 2), A_DIM, axis=2),
        qg=jnp.tile(b_q_gain[:, pair], (1, B_HEADS))[:, None, :],
        kg=jnp.tile(b_k_gain[:, pair], (1, B_KV_HEADS))[:, None, :],
        cqg=c_q_gain[:, None, :],
        w_uq=_reorder_w_uq(c_w_uq).astype(BF16),
        ckvg=c_kv_gain[:, None, :],
        w_ukv=_reorder_w_ukv(c_w_ukv).astype(BF16),
        n2g=norm2_gain[:, None, :],
        w_pa=w_pa.astype(BF16),
        w_pb=w_pb.astype(BF16),
        w_pc=w_pc.astype(BF16),
        w_out=w_out.astype(BF16),
        w_query=peer_w_query.astype(BF16),
        sub_keys=peer_sub_keys.reshape(depth, 2 * PEER_HEADS, PEER_N_KEYS, PEER_HALF).astype(BF16),
    )
    u_b = peer_u.astype(BF16)
    v_b = peer_v.astype(BF16)

    xf = x.reshape(t, d)
    for l in range(depth):
        ya, gates, qb, kb, vtb, qc, kc, vtc = _inproj_call(xf, mod_all, lw, tabs, l, tm, seq)
        yb = _attn_call(qb, kb, vtb, batch, seq, tq, True, B_HEADS // B_KV_HEADS)
        yc = _attn_call(qc, kc, vtc, batch, seq, tq, False, 1)
        x1, h2, lrow, e1, r2, e2 = _merge_call(ya, yb, yc, gates, xf, mod_all, lw, l, tm_merge, seq)
        xf = _expert_call(h2, lrow, e1, r2, e2, u_b, v_b, x1, mod_all, final_gain[None, :],
                          l, tm_peer, ec, seq, l == depth - 1)
    return xf.reshape(batch, seq, d)
```

```python
import functools

import numpy as np
import jax
import jax.numpy as jnp
from jax import lax
from jax.experimental import pallas as pl
from jax.experimental.pallas import tpu as pltpu

F32 = jnp.float32
BF16 = jnp.bfloat16

LANE = 128
SUBLANES = 8
PACKED_ROWS = 16
VMEM_LIMIT = 56 * 1024 * 1024

D_MODEL = 1024
GRID_W = 64
BLOCK = 128
EPS = 1e-6
ROPE_THETA = 10000.0
LOG2E = 1.4426950408889634

A_GROUPS = 8
A_DIM = 64
A_WIDTH = A_GROUPS * A_DIM
B_HEADS = 8
B_KV_HEADS = 2
B_HEAD_DIM = 64
C_HEADS = 8
C_NOPE = 64
C_ROPE = 32
C_V = 64
C_Q_RANK = 256
C_KV_RANK = 128
COLS_A = 2 * A_WIDTH
COLS_B = (B_HEADS + 2 * B_KV_HEADS) * B_HEAD_DIM
COLS_C = C_Q_RANK + C_KV_RANK + C_ROPE
N_BRANCH = 3
PEER_HEADS = 8
PEER_N_KEYS = 128
PEER_HALF = 128
PEER_TOPK = 16

OFF_A = 0
OFF_B = OFF_A + COLS_A
OFF_C = OFF_B + COLS_B
OFF_G = OFF_C + C_Q_RANK + C_KV_RANK + LANE
W_IN_COLS = OFF_G + N_BRANCH * D_MODEL

CAND_PAIRS = [(a, b) for a in range(PEER_TOPK) for b in range(PEER_TOPK) if (a + 1) * (b + 1) <= PEER_TOPK]


def _bitonic_merge_pairs(n):
    pairs, stride = [], n // 2
    while stride:
        pairs += [(i, i + stride) for i in range(n) if not i & stride]
        stride //= 2
    return pairs


def _sort_pairs(n):
    pairs = []

    def merge(lo, cnt, step):
        nxt = step * 2
        if nxt < cnt:
            merge(lo, cnt, nxt)
            merge(lo + step, cnt, nxt)
            pairs.extend((i, i + step) for i in range(lo + step, lo + cnt - step, nxt))
        else:
            pairs.append((lo, lo + step))

    def sort(lo, cnt):
        if cnt > 1:
            half = cnt // 2
            sort(lo, half)
            sort(lo + half, half)
            merge(lo, cnt, 1)

    sort(0, n)
    return pairs


SORT16 = _sort_pairs(PEER_TOPK)
BITONIC16 = _bitonic_merge_pairs(PEER_TOPK)


def _tile_sizes(seq):
    tm = min(512, seq)
    tq = min(2048, seq)
    tm_merge = min(256, seq)
    tm_peer = min(1024, seq)
    ec = 8 * PEER_N_KEYS
    assert seq % tm == 0 and seq % tq == 0 and seq % tm_peer == 0 and seq % tm_merge == 0
    return tm, tq, tm_merge, tm_peer, ec


def _cparams(sem):
    return pltpu.CompilerParams(dimension_semantics=sem, vmem_limit_bytes=VMEM_LIMIT)


def _const_spec(shape):
    nd = len(shape)
    return pl.BlockSpec(shape, lambda *_: (0,) * nd, pipeline_mode=pl.Buffered(1))


def _layer_spec(stacked, layer):
    shape = stacked.shape[1:]
    return pl.BlockSpec((None,) + shape, lambda *_: (layer,) + (0,) * len(shape), pipeline_mode=pl.Buffered(1))


def _gelu(x):
    return 0.5 * x * (1.0 + jnp.tanh(0.7978845608028654 * (x + 0.044715 * (x * x * x))))


def _rms_rows(x, gain):
    return x * lax.rsqrt(jnp.mean(x * x, axis=-1, keepdims=True) + EPS) * gain


def _group_sumsq(x, ones_blockdiag):
    sq = x * x
    hi = sq.astype(BF16)
    lo = (sq - hi.astype(F32)).astype(BF16)
    return (jnp.dot(hi, ones_blockdiag, preferred_element_type=F32)
            + jnp.dot(lo, ones_blockdiag, preferred_element_type=F32))


def _swap_halves(x, group, lo_mask):
    w = x.shape[-1]
    half = group // 2
    return jnp.where(lo_mask, pltpu.roll(x, w - half, 1), pltpu.roll(x, half, 1))


def _ada_kernel(c_ref, w_ref, b_ref, o_ref):
    c = c_ref[...]
    ca = c * jax.nn.sigmoid(c)
    o_ref[0] = jnp.dot(ca, w_ref[0], preferred_element_type=F32,
                       precision=lax.Precision.HIGHEST) + b_ref[0]


def _ada_call(c_pad, w_ada, b_ada):
    depth, d, n = w_ada.shape
    tn = n // 4
    return pl.pallas_call(
        _ada_kernel,
        grid=(depth, n // tn),
        in_specs=[
            pl.BlockSpec((SUBLANES, d), lambda l, j: (0, 0)),
            pl.BlockSpec((1, d, tn), lambda l, j: (l, 0, j)),
            pl.BlockSpec((1, 1, tn), lambda l, j: (l, 0, j)),
        ],
        out_specs=pl.BlockSpec((1, SUBLANES, tn), lambda l, j: (l, 0, j)),
        out_shape=jax.ShapeDtypeStruct((depth, SUBLANES, n), F32),
        compiler_params=_cparams(("arbitrary", "arbitrary")),
        name="ada_mod",
    )(c_pad, w_ada, b_ada.reshape(depth, 1, n))


def _inproj_kernel(x_ref, mod_ref, n1g_ref, win_ref, avg_ref, ws_ref, bmat_ref, ones_ref,
                   qg_ref, kg_ref, cosb_ref, sinb_ref, cqg_ref, wuq_ref, ckvg_ref, wukv_ref,
                   cosc_ref, sinc_ref,
                   ya_ref, gates_ref, qb_ref, kb_ref, vtb_ref, qc_ref, kc_ref, vtc_ref):
    tm = x_ref.shape[0]
    d = D_MODEL
    shift = mod_ref[0, 0:1, :]
    scale = mod_ref[0, 1:2, :]
    h = (_rms_rows(x_ref[...], n1g_ref[...]) * (1.0 + scale) + shift).astype(BF16)

    za = jnp.dot(h, win_ref[:, OFF_A:OFF_A + COLS_A], preferred_element_type=F32)
    z = _gelu(za)
    u = z[:, :A_WIDTH]
    vn = _rms_rows(z[:, A_WIDTH:], avg_ref[...]).astype(BF16)
    grp = lax.broadcasted_iota(jnp.int32, (BLOCK, A_WIDTH), 1) // A_DIM
    for c in range(tm // BLOCK):
        vc = vn[c * BLOCK:(c + 1) * BLOCK]
        mixed = jnp.zeros((BLOCK, A_WIDTH), F32)
        for g in range(A_GROUPS):
            r = jnp.dot(ws_ref[g], vc, preferred_element_type=F32)
            mixed = jnp.where(grp == g, r, mixed)
        ya_ref[c * BLOCK:(c + 1) * BLOCK, :] = (
            u[c * BLOCK:(c + 1) * BLOCK] * (mixed + bmat_ref[...])).astype(BF16)

    zb = jnp.dot(h, win_ref[:, OFF_B:OFF_B + COLS_B], preferred_element_type=F32)
    qw = B_HEADS * B_HEAD_DIM
    kw = B_KV_HEADS * B_HEAD_DIM
    cosb = cosb_ref[...]
    sinb = sinb_ref[...]
    lane_q = lax.broadcasted_iota(jnp.int32, (tm, qw), 1)
    q = zb[:, :qw]
    qn = q * lax.rsqrt(_group_sumsq(q, ones_ref[...]) * (1.0 / B_HEAD_DIM) + EPS) * qg_ref[...]
    reps = qw // LANE
    qr = (qn * jnp.concatenate([cosb] * reps, axis=1)
          + _swap_halves(qn, B_HEAD_DIM, (lane_q % B_HEAD_DIM) < B_HEAD_DIM // 2)
          * jnp.concatenate([sinb] * reps, axis=1)) * (B_HEAD_DIM ** -0.5 * LOG2E)
    lane_t = lax.broadcasted_iota(jnp.int32, (tm, LANE), 1)
    for t in range(reps):
        tile = qr[:, t * LANE:(t + 1) * LANE]
        qb_ref[:, t * LANE:(t + 1) * LANE] = jnp.where(lane_t < B_HEAD_DIM, tile, 0.0).astype(BF16)
        qb_ref[:, (reps + t) * LANE:(reps + t + 1) * LANE] = (
            jnp.where(lane_t >= B_HEAD_DIM, tile, 0.0).astype(BF16))
    k = zb[:, qw:qw + kw]
    kn = k * lax.rsqrt(_group_sumsq(k, ones_ref[0:kw, 0:kw]) * (1.0 / B_HEAD_DIM) + EPS) * kg_ref[...]
    kr = kn * cosb + _swap_halves(kn, B_HEAD_DIM, (lane_t % B_HEAD_DIM) < B_HEAD_DIM // 2) * sinb
    kb_ref[...] = kr.astype(BF16)
    vtb_ref[0] = zb[:, qw + kw:].T.astype(BF16)

    zc = jnp.dot(h, win_ref[:, OFF_C:OFF_G], preferred_element_type=F32)
    cosc = cosc_ref[...]
    sinc = sinc_ref[...]
    rope_lo = C_NOPE + C_ROPE // 2
    cqn = _rms_rows(zc[:, :C_Q_RANK], cqg_ref[...]).astype(BF16)
    qf = jnp.dot(cqn, wuq_ref[...], preferred_element_type=F32)
    lane_c = lax.broadcasted_iota(jnp.int32, qf.shape, 1)
    qrot = (qf * jnp.concatenate([cosc] * C_HEADS, axis=1)
            + _swap_halves(qf, C_ROPE, (lane_c % LANE) < rope_lo)
            * jnp.concatenate([sinc] * C_HEADS, axis=1)) * ((C_NOPE + C_ROPE) ** -0.5 * LOG2E)
    qc_ref[...] = qrot.astype(BF16)
    ckvn = _rms_rows(zc[:, C_Q_RANK:C_Q_RANK + C_KV_RANK], ckvg_ref[...]).astype(BF16)
    kv = jnp.dot(ckvn, wukv_ref[...], preferred_element_type=F32)
    krope = zc[:, C_Q_RANK + C_KV_RANK:]
    krot = krope * cosc + _swap_halves(krope, C_ROPE, lane_t < rope_lo) * sinc
    kc_ref[...] = (kv[:, :C_HEADS * LANE] + jnp.concatenate([krot] * C_HEADS, axis=1)).astype(BF16)
    vtc_ref[0] = kv[:, C_HEADS * LANE:].T.astype(BF16)

    for n in range(N_BRANCH):
        zg = jnp.dot(h, win_ref[:, OFF_G + n * d:OFF_G + (n + 1) * d], preferred_element_type=F32)
        gates_ref[:, n * d:(n + 1) * d] = jax.nn.sigmoid(zg).astype(BF16)


def _inproj_call(x, mod, lw, tabs, layer, tm, seq):
    t, d = x.shape
    nb = t // tm
    per_seq = seq // tm
    row = lambda i: (i, 0)
    pos = lambda i: (i % per_seq, 0)
    per_layer = lambda name: _layer_spec(lw[name], layer)
    in_specs = [
        pl.BlockSpec((tm, d), row),
        pl.BlockSpec((None, 1, 6, d), lambda i: (layer, i // per_seq, 0, 0)),
        per_layer("n1g"),
        per_layer("w_in"),
        per_layer("a_v_gain"),
        per_layer("w_s"),
        per_layer("bmat"),
        _const_spec((B_HEADS * B_HEAD_DIM, B_HEADS * B_HEAD_DIM)),
        per_layer("qg"),
        per_layer("kg"),
        pl.BlockSpec((tm, LANE), pos),
        pl.BlockSpec((tm, LANE), pos),
        per_layer("cqg"),
        per_layer("w_uq"),
        per_layer("ckvg"),
        per_layer("w_ukv"),
        pl.BlockSpec((tm, LANE), pos),
        pl.BlockSpec((tm, LANE), pos),
    ]
    out_shapes = (
        jax.ShapeDtypeStruct((t, A_WIDTH), BF16),
        jax.ShapeDtypeStruct((t, N_BRANCH * d), BF16),
        jax.ShapeDtypeStruct((t, B_HEADS * LANE), BF16),
        jax.ShapeDtypeStruct((t, LANE), BF16),
        jax.ShapeDtypeStruct((nb, B_KV_HEADS * B_HEAD_DIM, tm), BF16),
        jax.ShapeDtypeStruct((t, C_HEADS * LANE), BF16),
        jax.ShapeDtypeStruct((t, C_HEADS * LANE), BF16),
        jax.ShapeDtypeStruct((nb, C_HEADS * C_V, tm), BF16),
    )
    out_specs = (
        pl.BlockSpec((tm, A_WIDTH), row),
        pl.BlockSpec((tm, N_BRANCH * d), row),
        pl.BlockSpec((tm, B_HEADS * LANE), row),
        pl.BlockSpec((tm, LANE), row),
        pl.BlockSpec((1, B_KV_HEADS * B_HEAD_DIM, tm), lambda i: (i, 0, 0)),
        pl.BlockSpec((tm, C_HEADS * LANE), row),
        pl.BlockSpec((tm, C_HEADS * LANE), row),
        pl.BlockSpec((1, C_HEADS * C_V, tm), lambda i: (i, 0, 0)),
    )
    return pl.pallas_call(
        _inproj_kernel,
        grid=(nb,),
        in_specs=in_specs,
        out_specs=out_specs,
        out_shape=out_shapes,
        compiler_params=_cparams(("arbitrary",)),
        name="in_proj",
    )(x, mod, lw["n1g"], lw["w_in"], lw["a_v_gain"], lw["w_s"], lw["bmat"], tabs["ones"],
      lw["qg"], lw["kg"], tabs["cosb"], tabs["sinb"], lw["cqg"], lw["w_uq"], lw["ckvg"], lw["w_ukv"],
      tabs["cosc"], tabs["sinc"])


def _attn_kernel(q_ref, k_ref, vt_ref, o_ref, s_ref, *, heads_share_kv, dv):
    tq = q_ref.shape[0]
    n_slabs, _, slab = vt_ref.shape
    tk = s_ref.shape[2]
    group = tk // slab
    n_chunks = n_slabs // group
    ones = jnp.ones((PACKED_ROWS, tk), BF16)
    hs = range(2)
    qs = [q_ref[:, hh * LANE:(hh + 1) * LANE] for hh in hs]

    def scores(c, slot):
        start = pl.multiple_of(c * tk, tk)
        s_t = [lax.dot_general(k_ref[pl.ds(start, tk), (0 if heads_share_kv else hh * LANE):
                                     (LANE if heads_share_kv else (hh + 1) * LANE)],
                               qs[hh], (((1,), (1,)), ((), ())), preferred_element_type=F32) for hh in hs]
        for hh in hs:
            s_ref[slot, hh] = s_t[hh]
        return tuple(jnp.max(s_t[hh], axis=0, keepdims=True) for hh in hs)

    def accumulate(c, slot, state, cmax):
        m_new = [jnp.maximum(state[hh][0], cmax[hh]) for hh in hs]
        p = [jnp.exp2(s_ref[slot, hh] - m_new[hh]).astype(BF16) for hh in hs]
        alpha = [jnp.exp2(state[hh][0] - m_new[hh]) for hh in hs]
        vt = []
        for hh in hs:
            v_rows = slice(0, dv) if heads_share_kv else slice(hh * dv, (hh + 1) * dv)
            v_t = jnp.concatenate([vt_ref[c * group + g, v_rows, :] for g in range(group)], axis=1)
            vt.append(jnp.concatenate([v_t, ones], axis=0))
        pv = [jnp.dot(vt[hh], p[hh], preferred_element_type=F32) for hh in hs]
        return tuple((m_new[hh], state[hh][1] * alpha[hh] + pv[hh]) for hh in hs)

    def body(i, carry):
        state, cmax = carry
        c = 2 * i
        cmax1 = scores(c + 1, 1)
        state = accumulate(c, 0, state, cmax)
        cmax0 = scores(jnp.minimum(c + 2, n_chunks - 1), 0)
        state = accumulate(c + 1, 1, state, cmax1)
        return state, cmax0

    state = tuple((jnp.full((1, tq), -jnp.inf, F32), jnp.zeros((dv + PACKED_ROWS, tq), F32)) for _ in hs)
    cmax = scores(0, 0)
    if n_chunks > 1:
        assert n_chunks % 2 == 0
        state, _ = lax.fori_loop(0, n_chunks // 2, body, (state, cmax))
    else:
        state = accumulate(0, 0, state, cmax)
    outs = [acc[:dv] * (1.0 / acc[dv:dv + 1]) for _, acc in state]
    o_ref[...] = jnp.concatenate(outs, axis=0).T.astype(BF16)


def _attn_call(q, k, vt, batch, seq, tq, heads_share_kv, q_heads_per_kv):
    t = q.shape[0]
    n_heads = q.shape[1] // LANE
    tk = vt.shape[2]
    dv = 64
    n_chunks = seq // tk
    nq = seq // tq
    group = 2 if n_chunks % 4 == 0 else 1
    once = pl.Buffered(1)
    if heads_share_kv:
        k_spec = pl.BlockSpec((seq, LANE), lambda b, j, i: (b, 0), pipeline_mode=once)
        pairs_per_kv = q_heads_per_kv // 2
        vt_spec = pl.BlockSpec((n_chunks, dv, tk), lambda b, j, i: (b, j // pairs_per_kv, 0), pipeline_mode=once)
    else:
        k_spec = pl.BlockSpec((seq, 2 * LANE), lambda b, j, i: (b, j), pipeline_mode=once)
        vt_spec = pl.BlockSpec((n_chunks, 2 * dv, tk), lambda b, j, i: (b, j, 0), pipeline_mode=once)
    return pl.pallas_call(
        functools.partial(_attn_kernel, heads_share_kv=heads_share_kv, dv=dv),
        grid=(batch, n_heads // 2, nq),
        in_specs=[pl.BlockSpec((tq, 2 * LANE), lambda b, j, i: (b * nq + i, j)), k_spec, vt_spec],
        out_specs=pl.BlockSpec((tq, 2 * dv), lambda b, j, i: (b * nq + i, j)),
        out_shape=jax.ShapeDtypeStruct((t, n_heads * dv), BF16),
        compiler_params=_cparams(("arbitrary", "arbitrary", "arbitrary")),
        scratch_shapes=[pltpu.VMEM((2, 2, group * tk, tq), F32)],
        name="attn_shared_kv" if heads_share_kv else "attn_latent",
    )(q, k, vt)


def _merge_kernel(ya_ref, yb_ref, yc_ref, gates_ref, x_ref, mod_ref, n2g_ref, wpa_ref, wpb_ref,
                  wpc_ref, wout_ref, wq_ref, sk_ref,
                  x1_ref, h2_ref, lrow_ref, e1_ref, r2_ref, e2_ref, vals_ref, s_scr):
    d = D_MODEL
    merged = (gates_ref[:, 0:d].astype(F32) * jnp.dot(ya_ref[...], wpa_ref[...], preferred_element_type=F32)
              + gates_ref[:, d:2 * d].astype(F32) * jnp.dot(yb_ref[...], wpb_ref[...], preferred_element_type=F32)
              + gates_ref[:, 2 * d:3 * d].astype(F32) * jnp.dot(yc_ref[...], wpc_ref[...], preferred_element_type=F32))
    x1 = x_ref[...] + mod_ref[0, 2:3, :] * jnp.dot(merged.astype(BF16), wout_ref[...],
                                                   preferred_element_type=F32)
    x1_ref[...] = x1
    h2 = (_rms_rows(x1, n2g_ref[...]) * (1.0 + mod_ref[0, 4:5, :]) + mod_ref[0, 3:4, :]).astype(BF16)
    h2_ref[...] = h2
    qk = jnp.dot(h2, wq_ref[...], preferred_element_type=F32).astype(BF16)

    sub = SUBLANES
    n_tiles = PEER_N_KEYS // sub
    assert n_tiles == PEER_TOPK
    for h in range(PEER_HEADS):
        for p in range(2):
            idx = 2 * h + p
            s = lax.dot_general(sk_ref[idx], qk[:, idx * PEER_HALF:(idx + 1) * PEER_HALF],
                                (((1,), (1,)), ((), ())), preferred_element_type=F32)
            s_scr[p, h] = s
            for t0 in range(0, s.shape[1], LANE):
                tiles = [s[k * sub:(k + 1) * sub, t0:t0 + LANE] for k in range(n_tiles)]
                top = list(tiles)
                for i, j in SORT16:
                    top[i], top[j] = jnp.maximum(top[i], top[j]), jnp.minimum(top[i], top[j])
                for shift in (4, 2, 1):
                    other = [pltpu.roll(v, shift, 0) for v in top]
                    top = [jnp.maximum(top[k], other[n_tiles - 1 - k]) for k in range(n_tiles)]
                    for i, j in BITONIC16:
                        top[i], top[j] = jnp.maximum(top[i], top[j]), jnp.minimum(top[i], top[j])
                for a in range(PEER_TOPK):
                    vals_ref[p, a, h:h + 1, t0:t0 + LANE] = top[a][0:1, :]
                if p == 1:
                    for k in range(0, n_tiles, 2):
                        rank = [functools.reduce(jnp.add, [jnp.where(tiles[k + i] < top[a], 1.0, 0.0)
                                                           for a in range(PEER_TOPK)]) for i in range(2)]
                        r2_ref[h, k * sub:(k + 2) * sub, t0:t0 + LANE] = jnp.concatenate(rank, axis=0).astype(BF16)

    v1 = [vals_ref[0, a] for a in range(PEER_TOPK)]
    v2 = [vals_ref[1, a] for a in range(PEER_TOPK)]
    cands = [v1[a] + v2[b] for a, b in CAND_PAIRS]
    work = list(cands)
    tau = None
    for it in range(PEER_TOPK):
        mx = functools.reduce(jnp.maximum, work)
        if it == PEER_TOPK - 1:
            tau = mx
        else:
            work = [jnp.where(w == mx, -jnp.inf, w) for w in work]
    top = v1[0] + v2[0]
    zsum = functools.reduce(jnp.add, [jnp.where(c >= tau, jnp.exp(c - top), 0.0) for c in cands])
    rz = 1.0 / zsum
    count = [functools.reduce(jnp.add, [jnp.where(c >= tau, 1.0, 0.0)
                                         for c, (a2, _) in zip(cands, CAND_PAIRS) if a2 == a])
             for a in range(PEER_TOPK)]
    for h in range(PEER_HEADS):
        s1 = s_scr[0, h]
        lrow = jnp.zeros(s1.shape, F32)
        for a in range(PEER_TOPK):
            lrow = jnp.where(s1 == v1[a][h:h + 1, :], count[a][h:h + 1, :], lrow)
        lrow_ref[h] = lrow
        e1_ref[h] = jnp.exp(s1 - v1[0][h:h + 1, :])
        e2_ref[h] = (jnp.exp(s_scr[1, h] - v2[0][h:h + 1, :]) * rz[h:h + 1, :]).astype(BF16)


def _merge_call(ya, yb, yc, gates, x, mod, lw, layer, tm, seq):
    t, d = x.shape
    nb = t // tm
    per_seq = seq // tm
    row = lambda i: (i, 0)
    per_layer = lambda name: _layer_spec(lw[name], layer)
    hk = pl.BlockSpec((PEER_HEADS, PEER_N_KEYS, tm), lambda i: (0, 0, i))
    hk_f32 = jax.ShapeDtypeStruct((PEER_HEADS, PEER_N_KEYS, t), F32)
    hk_bf16 = jax.ShapeDtypeStruct((PEER_HEADS, PEER_N_KEYS, t), BF16)
    return pl.pallas_call(
        _merge_kernel,
        grid=(nb,),
        in_specs=[
            pl.BlockSpec((tm, A_WIDTH), row),
            pl.BlockSpec((tm, A_WIDTH), row),
            pl.BlockSpec((tm, A_WIDTH), row),
            pl.BlockSpec((tm, N_BRANCH * d), row),
            pl.BlockSpec((tm, d), row),
            pl.BlockSpec((None, 1, 6, d), lambda i: (layer, i // per_seq, 0, 0)),
            per_layer("n2g"),
            per_layer("w_pa"),
            per_layer("w_pb"),
            per_layer("w_pc"),
            per_layer("w_out"),
            per_layer("w_query"),
            per_layer("sub_keys"),
        ],
        out_specs=(
            pl.BlockSpec((tm, d), row),
            pl.BlockSpec((tm, d), row),
            hk, hk, hk, hk,
        ),
        out_shape=(
            jax.ShapeDtypeStruct((t, d), F32),
            jax.ShapeDtypeStruct((t, d), BF16),
            hk_f32, hk_f32, hk_bf16, hk_bf16,
        ),
        scratch_shapes=[pltpu.VMEM((2, PEER_TOPK, PEER_HEADS, tm), F32),
                        pltpu.VMEM((2, PEER_HEADS, PEER_N_KEYS, tm), F32)],
        compiler_params=_cparams(("arbitrary",)),
        name="merge_route",
    )(ya, yb, yc, gates, x, mod, lw["n2g"], lw["w_pa"], lw["w_pb"], lw["w_pc"], lw["w_out"],
      lw["w_query"], lw["sub_keys"])


def _expert_kernel(h2_ref, lrow_ref, e1_ref, r2_ref, e2_ref, u_ref, v_ref, x1_ref, mod_ref,
                   fg_ref, o_ref, acc_ref, w_ref, *, final_norm):
    ch = pl.program_id(1)
    rows = lrow_ref.shape[1]
    tm = h2_ref.shape[0]
    pack_rows = PACKED_ROWS

    @pl.when(ch == 0)
    def _():
        acc_ref[...] = jnp.zeros_like(acc_ref)

    def row_tile(ref, h, r):
        row = jnp.broadcast_to(ref[h, r:r + 1, :], (pack_rows, tm)).astype(BF16)
        return jnp.concatenate([row] * (PEER_N_KEYS // pack_rows), axis=0)

    for r in range(rows):
        gate = None
        for h in range(PEER_HEADS):
            sel = jnp.where(r2_ref[h] < row_tile(lrow_ref, h, r), e2_ref[h] * row_tile(e1_ref, h, r), 0.0)
            gate = sel if gate is None else gate + sel
        rs = slice(r * PEER_N_KEYS, (r + 1) * PEER_N_KEYS)
        w_ref[rs, :] = gate
    act = lax.dot_general(u_ref[...], h2_ref[...], (((1,), (1,)), ((), ())), preferred_element_type=F32)
    for r in range(rows):
        rs = slice(r * PEER_N_KEYS, (r + 1) * PEER_N_KEYS)
        w_ref[rs, :] = w_ref[rs, :] * _gelu(act[rs, :].astype(BF16))
    acc_ref[...] += lax.dot_general(v_ref[...], w_ref[...], (((0,), (0,)), ((), ())),
                                    preferred_element_type=F32)

    @pl.when(ch == pl.num_programs(1) - 1)
    def _():
        x2 = x1_ref[...] + mod_ref[0, 5:6, :] * acc_ref[...].T
        if final_norm:
            x2 = _rms_rows(x2, fg_ref[...])
        o_ref[...] = x2


def _expert_call(h2, lrow, e1, r2, e2, u, v, x1, mod, final_gain, layer, tm, ec, seq, final_norm):
    t, d = x1.shape
    n_exp = u.shape[1]
    rows = ec // PEER_N_KEYS
    per_seq = seq // tm
    tok = lambda i, c: (i, 0)
    chunk_rows = pl.BlockSpec((PEER_HEADS, rows, tm), lambda i, c: (0, c, i))
    all_rows = pl.BlockSpec((PEER_HEADS, PEER_N_KEYS, tm), lambda i, c: (0, 0, i))
    return pl.pallas_call(
        functools.partial(_expert_kernel, final_norm=final_norm),
        grid=(t // tm, n_exp // ec),
        in_specs=[
            pl.BlockSpec((tm, d), tok),
            chunk_rows, chunk_rows, all_rows, all_rows,
            pl.BlockSpec((None, ec, d), lambda i, c: (layer, c, 0)),
            pl.BlockSpec((None, ec, d), lambda i, c: (layer, c, 0)),
            pl.BlockSpec((tm, d), tok),
            pl.BlockSpec((None, 1, 6, d), lambda i, c: (layer, i // per_seq, 0, 0)),
            pl.BlockSpec((1, d), lambda i, c: (0, 0)),
        ],
        out_specs=pl.BlockSpec((tm, d), tok),
        out_shape=jax.ShapeDtypeStruct((t, d), F32),
        scratch_shapes=[pltpu.VMEM((d, tm), F32), pltpu.VMEM((ec, tm), BF16)],
        compiler_params=_cparams(("arbitrary", "arbitrary")),
        name="peer_dense",
    )(h2, lrow, e1, r2, e2, u, v, x1, mod, final_gain)


def _pairs_last(w):
    n = w.shape[-1] // 2
    return jnp.swapaxes(w.reshape(w.shape[:-1] + (n, 2)), -1, -2).reshape(w.shape)


def _reorder_w_in(w):
    lead = w.shape[:-1]
    qw, kw = B_HEADS * B_HEAD_DIM, B_KV_HEADS * B_HEAD_DIM
    tiles = qw // LANE
    a = w[..., :COLS_A]
    q = _pairs_last(w[..., COLS_A:COLS_A + qw].reshape(lead + (2, tiles, B_HEAD_DIM)))
    q = jnp.swapaxes(q, -3, -2).reshape(lead + (qw,))
    k = _pairs_last(w[..., COLS_A + qw:COLS_A + qw + kw].reshape(lead + (B_KV_HEADS, B_HEAD_DIM))).reshape(lead + (kw,))
    v = w[..., COLS_A + qw + kw:COLS_A + COLS_B]
    base_c = COLS_A + COLS_B
    lowrank = w[..., base_c:base_c + C_Q_RANK + C_KV_RANK]
    rope = _pairs_last(w[..., base_c + C_Q_RANK + C_KV_RANK:base_c + COLS_C])
    gates = w[..., base_c + COLS_C:]
    zeros = lambda n: jnp.zeros(lead + (n,), w.dtype)
    return jnp.concatenate([a, q, k, v, lowrank, zeros(C_NOPE), rope, zeros(LANE - C_NOPE - C_ROPE), gates], axis=-1)


def _reorder_w_uq(w):
    lead = w.shape[:-1]
    w = w.reshape(lead + (C_HEADS, C_NOPE + C_ROPE))
    pad = jnp.zeros(lead + (C_HEADS, LANE - C_NOPE - C_ROPE), w.dtype)
    return jnp.concatenate([w[..., :C_NOPE], _pairs_last(w[..., C_NOPE:]), pad], axis=-1).reshape(lead + (C_HEADS * LANE,))


def _reorder_w_ukv(w):
    lead = w.shape[:-1]
    w = w.reshape(lead + (C_HEADS, C_NOPE + C_V))
    pad = jnp.zeros(lead + (C_HEADS, LANE - C_NOPE), w.dtype)
    k = jnp.concatenate([w[..., :C_NOPE], pad], axis=-1).reshape(lead + (C_HEADS * LANE,))
    return jnp.concatenate([k, w[..., C_NOPE:].reshape(lead + (C_HEADS * C_V,))], axis=-1)


def _rope_tables(seq):
    rows = seq // GRID_W
    row = jnp.repeat(jnp.arange(rows, dtype=F32), GRID_W)
    col = jnp.tile(jnp.arange(GRID_W, dtype=F32), rows)

    def cos_sin(d_rot):
        half = d_rot // 2
        freq = ROPE_THETA ** (-jnp.arange(0, half, 2, dtype=F32) / half)
        ang = jnp.concatenate([row[:, None] * freq, col[:, None] * freq], axis=-1)
        return jnp.cos(ang), jnp.sin(ang)

    cb, sb = cos_sin(B_HEAD_DIM)
    cosb = jnp.tile(jnp.concatenate([cb, cb], axis=1), (1, LANE // B_HEAD_DIM))
    sinb = jnp.tile(jnp.concatenate([-sb, sb], axis=1), (1, LANE // B_HEAD_DIM))
    cc, sc = cos_sin(C_ROPE)
    one = jnp.ones((seq, C_NOPE), F32)
    pad = LANE - C_NOPE - C_ROPE
    cosc = jnp.concatenate([one, cc, cc, jnp.ones((seq, pad), F32)], axis=1)
    sinc = jnp.concatenate([0.0 * one, -sc, sc, jnp.zeros((seq, pad), F32)], axis=1)
    return cosb, sinb, cosc, sinc


def kernel(x, c, w_ada, b_ada, norm1_gain, w_in, a_v_gain, a_w_s, a_b_s, b_q_gain, b_k_gain,
           c_q_gain, c_w_uq, c_kv_gain, c_w_ukv, w_pa, w_pb, w_pc, w_out, norm2_gain,
           peer_w_query, peer_sub_keys, peer_u, peer_v, final_gain):
    batch, seq, d = x.shape
    depth = w_ada.shape[0]
    t = batch * seq
    assert d == D_MODEL and seq % GRID_W == 0
    tm, tq, tm_merge, tm_peer, ec = _tile_sizes(seq)

    cosb, sinb, cosc, sinc = _rope_tables(seq)
    qw = B_HEADS * B_HEAD_DIM
    ones = jnp.asarray(np.kron(np.eye(qw // B_HEAD_DIM), np.ones((B_HEAD_DIM, B_HEAD_DIM))), BF16)
    tabs = dict(cosb=cosb, sinb=sinb, cosc=cosc, sinc=sinc, ones=ones)

    assert batch <= SUBLANES and x.dtype == F32 and c.shape == (batch, d)
    c_pad = jnp.zeros((SUBLANES, d), F32).at[:batch].set(c)
    mod_all = _ada_call(c_pad, w_ada, b_ada)[:, :batch, :].reshape(depth, batch, 6, d)

    pair = np.concatenate([np.arange(0, B_HEAD_DIM, 2), np.arange(1, B_HEAD_DIM, 2)])
    lw = dict(
        n1g=norm1_gain[:, None, :],
        w_in=_reorder_w_in(w_in).astype(BF16),
        a_v_gain=a_v_gain[:, None, :],
        w_s=a_w_s.astype(BF16),
        bmat=jnp.repeat(jnp.swapaxes(a_b_s, 1, 2), A_DIM, axis=2),
        qg=jnp.tile(b_q_gain[:, pair], (1, B_HEADS))[:, None, :],
        kg=jnp.tile(b_k_gain[:, pair], (1, B_KV_HEADS))[:, None, :],
        cqg=c_q_gain[:, None, :],
        w_uq=_reorder_w_uq(c_w_uq).astype(BF16),
        ckvg=c_kv_gain[:, None, :],
        w_ukv=_reorder_w_ukv(c_w_ukv).astype(BF16),
        n2g=norm2_gain[:, None, :],
        w_pa=w_pa.astype(BF16),
        w_pb=w_pb.astype(BF16),
        w_pc=w_pc.astype(BF16),
        w_out=w_out.astype(BF16),
        w_query=peer_w_query.astype(BF16),
        sub_keys=peer_sub_keys.reshape(depth, 2 * PEER_HEADS, PEER_N_KEYS, PEER_HALF).astype(BF16),
    )
    u_b = peer_u.astype(BF16)
    v_b = peer_v.astype(BF16)

    xf = x.reshape(t, d)
    for l in range(depth):
        ya, gates, qb, kb, vtb, qc, kc, vtc = _inproj_call(xf, mod_all, lw, tabs, l, tm, seq)
        yb = _attn_call(qb, kb, vtb, batch, seq, tq, True, B_HEADS // B_KV_HEADS)
        yc = _attn_call(qc, kc, vtc, batch, seq, tq, False, 1)
        x1, h2, lrow, e1, r2, e2 = _merge_call(ya, yb, yc, gates, xf, mod_all, lw, l, tm_merge, seq)
        xf = _expert_call(h2, lrow, e1, r2, e2, u_b, v_b, x1, mod_all, final_gain[None, :],
                          l, tm_peer, ec, seq, l == depth - 1)
    return xf.reshape(batch, seq, d)
```
